```python
import math
import jax, jax.numpy as jnp
from jax import lax
import numpy as np

D_MODEL = 1024
BATCH = 2
SEQ = 8192
DEPTH = 4
DEC_BATCH = 8
DEC_SEQ = 32
PAST_LEN = 1024

CHUNK = 64
N_MIXERS = 2
N_A_LAYERS = (DEPTH + 1) // 2
N_B_LAYERS = DEPTH // 2
D_MIX = D_MODEL
S5_GROUP = 16
S5_GROUPS = D_MIX // S5_GROUP
S5_STATE = 64
POOL_WINDOWS = (2, 4, 8, 16)
POOL_GROUPS = len(POOL_WINDOWS)
POOL_GW = D_MIX // POOL_GROUPS
POOL_HIST = max(POOL_WINDOWS) - 1
D_FF = 2816
N_MEM = 256
N_XHEADS = 4
XHEAD_DIM = D_MODEL // N_XHEADS
N_SUB = 4
RMS_EPS = 1e-6

kernel_name = "s5_pool_macaron_stream_step"


def rmsnorm(x, g):
    xf = x.astype(jnp.float32)
    y = xf * lax.rsqrt(jnp.mean(xf * xf, axis=-1, keepdims=True) + RMS_EPS)
    return (y * g.astype(jnp.float32)).astype(x.dtype)


def swiglu(h, w_gate, w_up, w_down):
    return (jax.nn.silu(h @ w_gate) * (h @ w_up)) @ w_down


def s5_discretise(a_re, a_im, b_re, b_im, log_dt):
    f32 = jnp.float32
    lam = lax.complex(a_re.astype(f32), a_im.astype(f32))
    dt = jnp.exp(log_dt.astype(f32))[:, None]
    lam_bar = jnp.exp(lam * dt)
    b = lax.complex(b_re.astype(f32), b_im.astype(f32))
    b_bar = ((lam_bar - 1.0) / lam)[:, :, None] * b
    return lam_bar, b_bar


def _linear_combine(left, right):
    a_l, b_l = left
    a_r, b_r = right
    return a_r * a_l, a_r * b_l + b_r


def s5_block(h0, u_blk, lam_bar, b_bar, c):
    bu = jnp.einsum('gpc,btgc->btgp', b_bar, u_blk.astype(jnp.complex64))
    a = jnp.broadcast_to(lam_bar, bu.shape)
    a_cum, h_loc = lax.associative_scan(_linear_combine, (a, bu), axis=1)
    h = a_cum * h0[:, None] + h_loc
    y = jnp.einsum('gcp,btgp->btgc', c, h).real
    return h[:, -1], y


def s5_mixer(u, h0, a_re, a_im, b_re, b_im, c_re, c_im, d, log_dt, w_glu):
    f32 = jnp.float32
    bsz, t, _ = u.shape
    lam_bar, b_bar = s5_discretise(a_re, a_im, b_re, b_im, log_dt)
    c = lax.complex(c_re.astype(f32), c_im.astype(f32))
    ug = u.astype(f32).reshape(bsz, t, S5_GROUPS, S5_GROUP)
    if t > CHUNK:
        nc = t // CHUNK
        blocks = ug.reshape(bsz, nc, CHUNK, S5_GROUPS, S5_GROUP).transpose(1, 0, 2, 3, 4)

        def step(h, ub):
            return s5_block(h, ub, lam_bar, b_bar, c)

        h_last, ys = lax.scan(step, h0, blocks)
        y = ys.transpose(1, 0, 2, 3, 4).reshape(bsz, t, D_MIX)
    else:
        h_last, y = s5_block(h0, ug, lam_bar, b_bar, c)
        y = y.reshape(bsz, t, D_MIX)
    y = y + d.astype(f32) * u.astype(f32)
    z = jax.nn.gelu(y).astype(u.dtype)
    gate = z @ w_glu
    out = gate[..., :D_MODEL] * jax.nn.sigmoid(gate[..., D_MODEL:])
    return out, h_last


def pool_mixer(u, hist, pos0, w_grp, b_grp, scale):
    f32 = jnp.float32
    bsz, t, _ = u.shape
    ext = jnp.concatenate([hist.astype(u.dtype), u], axis=1)
    cs = jnp.cumsum(ext.astype(f32), axis=1)
    cs = jnp.concatenate([jnp.zeros((bsz, 1, D_MIX), f32), cs], axis=1)
    pos = pos0 + jnp.arange(t, dtype=jnp.int32)
    end = cs[:, POOL_HIST + 1:POOL_HIST + 1 + t]
    parts = []
    for g, w in enumerate(POOL_WINDOWS):
        sl = slice(g * POOL_GW, (g + 1) * POOL_GW)
        start = cs[:, POOL_HIST + 1 - w:POOL_HIST + 1 - w + t, sl]
        cnt = jnp.minimum(pos + 1, w).astype(f32)[None, :, None]
        parts.append((end[..., sl] - start) / cnt)
    pooled = jnp.concatenate(parts, axis=-1) - u.astype(f32)
    pooled = pooled.reshape(bsz, t, POOL_GROUPS, POOL_GW).astype(u.dtype)
    mixed = jnp.einsum('btgi,gio->btgo', pooled, w_grp) + b_grp
    out = mixed.reshape(bsz, t, D_MIX) * scale
    return out, ext[:, -POOL_HIST:]


def cross_attend(h, k, v, w_q, w_o):
    bsz, t, _ = h.shape
    q = (h @ w_q).reshape(bsz, t, N_XHEADS, XHEAD_DIM)
    s = jnp.einsum('bthd,bmhd->bhtm', q, k).astype(jnp.float32) * (XHEAD_DIM ** -0.5)
    p = jax.nn.softmax(s, axis=-1).astype(v.dtype)
    o = jnp.einsum('bhtm,bmhd->bthd', p, v).reshape(bsz, t, D_MODEL)
    return o @ w_o


def trunk(x, pos0, mem_k, mem_v, ssm_h0, pool_hist, p):
    new_h, new_buf = [], []
    for i in range(DEPTH):
        j = i // N_MIXERS
        h = rmsnorm(x, p['norm_pre'][i, 0])
        f = swiglu(h, p['ffn_w_gate'][i, 0], p['ffn_w_up'][i, 0], p['ffn_w_down'][i, 0])
        x = x + 0.5 * rmsnorm(f, p['norm_post'][i, 0])
        h = rmsnorm(x, p['norm_pre'][i, 1])
        if i % N_MIXERS == 0:
            mix, hs = s5_mixer(h @ p['s5_w_in'][j], ssm_h0[j], p['s5_a_re'][j], p['s5_a_im'][j],
                               p['s5_b_re'][j], p['s5_b_im'][j], p['s5_c_re'][j], p['s5_c_im'][j],
                               p['s5_d'][j], p['s5_log_dt'][j], p['s5_w_glu'][j])
            new_h.append(hs)
        else:
            mix, buf = pool_mixer(h @ p['pool_w_in'][j], pool_hist[j], pos0,
                                  p['pool_w_grp'][j], p['pool_b_grp'][j], p['pool_scale'][j])
            new_buf.append(buf)
        x = x + rmsnorm(mix, p['norm_post'][i, 1])
        h = rmsnorm(x, p['norm_pre'][i, 2])
        a = cross_attend(h, mem_k[i], mem_v[i], p['xa_w_q'][i], p['xa_w_o'][i])
        x = x + rmsnorm(a, p['norm_post'][i, 2])
        h = rmsnorm(x, p['norm_pre'][i, 3])
        f = swiglu(h, p['ffn_w_gate'][i, 1], p['ffn_w_up'][i, 1], p['ffn_w_down'][i, 1])
        x = x + 0.5 * rmsnorm(f, p['norm_post'][i, 3])
    return x, jnp.stack(new_h), jnp.stack(new_buf)


def setup_inputs(seed: int = 0) -> dict:
    key = jax.random.key(seed)
    ks = iter(jax.random.split(key, 40))
    f32 = jnp.float32

    def nrm(shape, scale=1.0):
        return jax.random.normal(next(ks), shape, f32) * scale

    n_idx = jnp.arange(S5_STATE, dtype=f32)
    inputs = {}
    inputs['x_prompt'] = nrm((BATCH, SEQ, D_MODEL))
    inputs['x_sample'] = nrm((DEC_BATCH, DEC_SEQ, D_MODEL))
    inputs['cache_mem_k'] = nrm((DEPTH, DEC_BATCH, N_MEM, N_XHEADS, XHEAD_DIM))
    inputs['cache_mem_v'] = nrm((DEPTH, DEC_BATCH, N_MEM, N_XHEADS, XHEAD_DIM))
    inputs['state_ssm_re'] = nrm((N_A_LAYERS, DEC_BATCH, S5_GROUPS, S5_STATE), 0.5)
    inputs['state_ssm_im'] = nrm((N_A_LAYERS, DEC_BATCH, S5_GROUPS, S5_STATE), 0.5)
    inputs['state_pool'] = nrm((N_A_LAYERS * 0 + N_B_LAYERS, DEC_BATCH, POOL_HIST, D_MIX))
    inputs['mem_prompt'] = nrm((BATCH, N_MEM, D_MODEL))
    inputs['norm_pre'] = 1.0 + nrm((DEPTH, N_SUB, D_MODEL), 0.05)
    inputs['norm_post'] = 1.0 + nrm((DEPTH, N_SUB, D_MODEL), 0.05)
    inputs['norm_mem'] = 1.0 + nrm((DEPTH, D_MODEL), 0.05)
    inputs['ffn_w_gate'] = nrm((DEPTH, 2, D_MODEL, D_FF), D_MODEL ** -0.5)
    inputs['ffn_w_up'] = nrm((DEPTH, 2, D_MODEL, D_FF), D_MODEL ** -0.5)
    inputs['ffn_w_down'] = nrm((DEPTH, 2, D_FF, D_MODEL), D_FF ** -0.5)
    inputs['xa_w_q'] = nrm((DEPTH, D_MODEL, D_MODEL), D_MODEL ** -0.5)
    inputs['xa_w_k'] = nrm((DEPTH, D_MODEL, D_MODEL), D_MODEL ** -0.5)
    inputs['xa_w_v'] = nrm((DEPTH, D_MODEL, D_MODEL), D_MODEL ** -0.5)
    inputs['xa_w_o'] = nrm((DEPTH, D_MODEL, D_MODEL), D_MODEL ** -0.5)
    inputs['s5_w_in'] = nrm((N_A_LAYERS, D_MODEL, D_MIX), D_MODEL ** -0.5)
    inputs['s5_a_re'] = -0.5 + nrm((N_A_LAYERS, S5_GROUPS, S5_STATE), 0.01)
    inputs['s5_a_im'] = jnp.broadcast_to(math.pi * n_idx, (N_A_LAYERS, S5_GROUPS, S5_STATE)) + nrm((N_A_LAYERS, S5_GROUPS, S5_STATE), 0.01)
    inputs['s5_b_re'] = nrm((N_A_LAYERS, S5_GROUPS, S5_STATE, S5_GROUP), (2 * S5_GROUP) ** -0.5)
    inputs['s5_b_im'] = nrm((N_A_LAYERS, S5_GROUPS, S5_STATE, S5_GROUP), (2 * S5_GROUP) ** -0.5)
    inputs['s5_c_re'] = nrm((N_A_LAYERS, S5_GROUPS, S5_GROUP, S5_STATE), (2 * S5_STATE) ** -0.5)
    inputs['s5_c_im'] = nrm((N_A_LAYERS, S5_GROUPS, S5_GROUP, S5_STATE), (2 * S5_STATE) ** -0.5)
    inputs['s5_d'] = nrm((N_A_LAYERS, D_MIX))
    inputs['s5_log_dt'] = jax.random.uniform(next(ks), (N_A_LAYERS, S5_GROUPS), f32, math.log(1e-3), math.log(1e-1))
    inputs['s5_w_glu'] = nrm((N_A_LAYERS, D_MIX, 2 * D_MODEL), D_MIX ** -0.5)
    inputs['pool_w_in'] = nrm((N_B_LAYERS, D_MODEL, D_MIX), D_MODEL ** -0.5)
    inputs['pool_w_grp'] = nrm((N_B_LAYERS, POOL_GROUPS, POOL_GW, POOL_GW), POOL_GW ** -0.5)
    inputs['pool_b_grp'] = nrm((N_B_LAYERS, POOL_GROUPS, POOL_GW), 0.02)
    inputs['pool_scale'] = 1.0 + nrm((N_B_LAYERS, D_MIX), 0.1)
    return inputs


def reference(x_prompt, x_sample, cache_mem_k, cache_mem_v, state_ssm_re, state_ssm_im, state_pool, mem_prompt,
              norm_pre, norm_post, norm_mem, ffn_w_gate, ffn_w_up, ffn_w_down,
              xa_w_q, xa_w_k, xa_w_v, xa_w_o,
              s5_w_in, s5_a_re, s5_a_im, s5_b_re, s5_b_im, s5_c_re, s5_c_im, s5_d, s5_log_dt, s5_w_glu,
              pool_w_in, pool_w_grp, pool_b_grp, pool_scale):
    f32 = jnp.float32
    p = dict(norm_pre=norm_pre, norm_post=norm_post, ffn_w_gate=ffn_w_gate, ffn_w_up=ffn_w_up,
             ffn_w_down=ffn_w_down, xa_w_q=xa_w_q, xa_w_o=xa_w_o, s5_w_in=s5_w_in, s5_a_re=s5_a_re,
             s5_a_im=s5_a_im, s5_b_re=s5_b_re, s5_b_im=s5_b_im, s5_c_re=s5_c_re, s5_c_im=s5_c_im,
             s5_d=s5_d, s5_log_dt=s5_log_dt, s5_w_glu=s5_w_glu, pool_w_in=pool_w_in,
             pool_w_grp=pool_w_grp, pool_b_grp=pool_b_grp, pool_scale=pool_scale)
    bsz = x_prompt.shape[0]

    m = rmsnorm(mem_prompt[None], norm_mem[:, None, None, :])
    mem_k_prompt = jnp.einsum('lbmd,lde->lbme', m, xa_w_k).reshape(DEPTH, bsz, N_MEM, N_XHEADS, XHEAD_DIM)
    mem_v_prompt = jnp.einsum('lbmd,lde->lbme', m, xa_w_v).reshape(DEPTH, bsz, N_MEM, N_XHEADS, XHEAD_DIM)
    h0_prompt = jnp.zeros((N_A_LAYERS, bsz, S5_GROUPS, S5_STATE), jnp.complex64)
    hist_prompt = jnp.zeros((N_B_LAYERS, bsz, POOL_HIST, D_MIX), x_prompt.dtype)
    y_prompt, h_p, buf_p = trunk(x_prompt, 0, mem_k_prompt, mem_v_prompt, h0_prompt, hist_prompt, p)

    h0_sample = lax.complex(state_ssm_re.astype(f32), state_ssm_im.astype(f32))
    y_sample, h_s, buf_s = trunk(x_sample, PAST_LEN, cache_mem_k, cache_mem_v, h0_sample, state_pool, p)

    sdt = state_ssm_re.dtype
    return (y_prompt, y_sample, mem_k_prompt, mem_v_prompt,
            jnp.real(h_p).astype(sdt), jnp.imag(h_p).astype(sdt), buf_p,
            jnp.real(h_s).astype(sdt), jnp.imag(h_s).astype(sdt), buf_s)
```

```python
import functools

import jax
import jax.numpy as jnp
from jax import lax
from jax.experimental import pallas as pl
from jax.experimental.pallas import tpu as pltpu

F32 = jnp.float32
BF16 = jnp.bfloat16

RMS_EPS = 1e-6
PAST_LEN = 1024
POOL_WINDOWS = (2, 4, 8, 16)
POOL_HIST = max(POOL_WINDOWS) - 1
HIST_ROWS = 16

LANES = 128
SUBLANES = 8
S5_CHUNK = LANES
VMEM_LIMIT = 56 * 1024 * 1024


def _params(n_axes=1, vmem=VMEM_LIMIT):
    return pltpu.CompilerParams(
        dimension_semantics=("arbitrary",) * n_axes, vmem_limit_bytes=vmem)


def _rms(x, g):
    ms = jnp.mean(x * x, axis=-1, keepdims=True)
    return x * lax.rsqrt(ms + RMS_EPS) * g


def _dot(a, b):
    return jnp.dot(a, b, preferred_element_type=F32)


def _dot_hi(a, b):
    return jnp.dot(a, b, preferred_element_type=F32, precision=lax.Precision.HIGHEST)


def _ffn_body(x_ref, gpre_ref, gpost_ref, wg_ref, wu_ref, wd_ref, o_ref):
    x = x_ref[...]
    hn = _rms(x, gpre_ref[0]).astype(BF16)
    g = _dot(hn, wg_ref[0])
    u = _dot(hn, wu_ref[0])
    a = (g * jax.nn.sigmoid(g) * u).astype(BF16)
    f = _dot(a, wd_ref[0])
    o_ref[...] = x + 0.5 * _rms(f, gpost_ref[0])


def _ffn(x, npre, npost, wg, wu, wd, layer, which, tm):
    nt, d = x.shape
    f = wg.shape[-1]
    sub = 0 if which == 0 else 3
    nsub = npre.shape[0] // (wg.shape[0] // 2)
    widx = layer * 2 + which
    nidx = layer * nsub + sub
    return pl.pallas_call(
        _ffn_body,
        grid=(nt // tm,),
        in_specs=[
            pl.BlockSpec((tm, d), lambda i: (i, 0)),
            pl.BlockSpec((1, 1, d), lambda i: (nidx, 0, 0)),
            pl.BlockSpec((1, 1, d), lambda i: (nidx, 0, 0)),
            pl.BlockSpec((1, d, f), lambda i: (widx, 0, 0)),
            pl.BlockSpec((1, d, f), lambda i: (widx, 0, 0)),
            pl.BlockSpec((1, f, d), lambda i: (widx, 0, 0)),
        ],
        out_specs=pl.BlockSpec((tm, d), lambda i: (i, 0)),
        out_shape=jax.ShapeDtypeStruct(x.shape, x.dtype),
        compiler_params=_params(),
        name="ffn",
    )(x, npre, npost, wg, wu, wd)


def _kv_body(m_ref, g_ref, wk_ref, wv_ref, k_ref, v_ref):
    m = _rms(m_ref[0], g_ref[0]).astype(BF16)
    k_ref[0, 0] = _dot(m, wk_ref[0])
    v_ref[0, 0] = _dot(m, wv_ref[0])


def _mem_kv(mem, nmem, wk, wv):
    b, nm, d = mem.shape
    depth = wk.shape[0]
    out = jax.ShapeDtypeStruct((depth, b, nm, d), F32)
    return pl.pallas_call(
        _kv_body,
        grid=(depth, b),
        in_specs=[
            pl.BlockSpec((1, nm, d), lambda l, j: (j, 0, 0)),
            pl.BlockSpec((1, 1, d), lambda l, j: (l, 0, 0)),
            pl.BlockSpec((1, d, d), lambda l, j: (l, 0, 0)),
            pl.BlockSpec((1, d, d), lambda l, j: (l, 0, 0)),
        ],
        out_specs=[pl.BlockSpec((1, 1, nm, d), lambda l, j: (l, j, 0, 0))] * 2,
        out_shape=[out, out],
        compiler_params=_params(2),
        name="mem_kv",
    )(mem, nmem, wk, wv)


def _attn_body(x_ref, gpre_ref, gpost_ref, wq_ref, wo_ref, k_ref, v_ref, o_ref, *, nh):
    x = x_ref[...]
    d = x.shape[-1]
    hd = d // nh
    hn = _rms(x, gpre_ref[0]).astype(BF16)
    q = (_dot(hn, wq_ref[0]) * (hd ** -0.5)).astype(BF16)
    heads = []
    for h in range(nh):
        sl = slice(h * hd, (h + 1) * hd)
        kh = k_ref[0, 0, :, sl].astype(BF16)
        vh = v_ref[0, 0, :, sl].astype(BF16)
        s = lax.dot_general(q[:, sl], kh, (((1,), (1,)), ((), ())), preferred_element_type=F32)
        e = jnp.exp(s - jnp.max(s, axis=-1, keepdims=True))
        p = e / jnp.sum(e, axis=-1, keepdims=True)
        heads.append(_dot(p.astype(BF16), vh))
    o = jnp.concatenate(heads, axis=-1).astype(BF16)
    o_ref[...] = x + _rms(_dot(o, wo_ref[0]), gpost_ref[0])


def _attn(x, npre, npost, wq, wo, k, v, layer, nh, row0, rows_per_batch, ta):
    nt, d = x.shape
    depth, nb, nm, _ = k.shape
    nsub = npre.shape[0] // depth
    nidx = layer * nsub + 2
    t0 = row0 // ta
    tpb = rows_per_batch // ta
    return pl.pallas_call(
        functools.partial(_attn_body, nh=nh),
        grid=(nb * tpb,),
        in_specs=[
            pl.BlockSpec((ta, d), lambda i: (t0 + i, 0)),
            pl.BlockSpec((1, 1, d), lambda i: (nidx, 0, 0)),
            pl.BlockSpec((1, 1, d), lambda i: (nidx, 0, 0)),
            pl.BlockSpec((1, d, d), lambda i: (layer, 0, 0)),
            pl.BlockSpec((1, d, d), lambda i: (layer, 0, 0)),
            pl.BlockSpec((1, 1, nm, d), lambda i: (layer, i // tpb, 0, 0)),
            pl.BlockSpec((1, 1, nm, d), lambda i: (layer, i // tpb, 0, 0)),
        ],
        out_specs=pl.BlockSpec((ta, d), lambda i: (t0 + i, 0)),
        out_shape=jax.ShapeDtypeStruct(x.shape, x.dtype),
        input_output_aliases={0: 0},
        compiler_params=_params(),
        name="xattn",
    )(x, npre, npost, wq, wo, k, v)


def _pool_body(x_ref, gpre_ref, gpost_ref, win_ref, wgrp_ref, bgrp_ref, scale_ref, hist_ref,
               o_ref, buf_ref, ext_ref, *, tiles_per_batch, pos0, valid_rows):
    i = pl.program_id(0)
    tt, d = x_ref.shape
    ng = len(POOL_WINDOWS)
    gw = d // ng
    first = (i % tiles_per_batch) == 0

    x = x_ref[...]
    hn = _rms(x, gpre_ref[0]).astype(BF16)
    u = _dot(hn, win_ref[0])

    @pl.when(first)
    def _():
        ext_ref[0:HIST_ROWS, :] = hist_ref[0, 0]

    @pl.when(jnp.logical_not(first))
    def _():
        ext_ref[0:HIST_ROWS, :] = ext_ref[tt:tt + HIST_ROWS, :]

    ext_ref[HIST_ROWS:HIST_ROWS + tt, :] = u

    tile_pos = pos0 + (i % tiles_per_batch) * tt
    pos = tile_pos + lax.broadcasted_iota(jnp.int32, (tt, 1), 0)
    level = ext_ref[...]
    parts = []
    for g, w in enumerate(POOL_WINDOWS):
        level = level + pltpu.roll(level, w // 2, 0)
        cnt = jnp.minimum(pos + 1, w).astype(F32)
        parts.append(level[HIST_ROWS:HIST_ROWS + tt, 0:gw] * (1.0 / cnt))
        if g + 1 < ng:
            level = level[:, gw:]
    pooled = (jnp.concatenate(parts, axis=-1) - u).astype(BF16)

    mixed = []
    for g in range(ng):
        sl = slice(g * gw, (g + 1) * gw)
        mixed.append(_dot(pooled[:, sl], wgrp_ref[0, g]) + bgrp_ref[0, :, sl])
    out = jnp.concatenate(mixed, axis=-1) * scale_ref[0]
    o_ref[...] = x + _rms(out, gpost_ref[0])

    @pl.when((i % tiles_per_batch) == tiles_per_batch - 1)
    def _():
        buf_ref[0] = u[valid_rows - POOL_HIST:valid_rows, :]


def _pool(x, npre, npost, win, wgrp, bgrp, scale, hist, layer, j, row0, rows_per_batch,
          valid_rows_last, pos0, tt):
    nt, d = x.shape
    nb = hist.shape[1]
    nidx = layer * 4 + 1
    ng = len(POOL_WINDOWS)
    gw = d // ng
    t0 = row0 // tt
    tpb = rows_per_batch // tt
    body = functools.partial(_pool_body, tiles_per_batch=tpb, pos0=pos0, valid_rows=valid_rows_last)
    return pl.pallas_call(
        body,
        grid=(nb * tpb,),
        in_specs=[
            pl.BlockSpec((tt, d), lambda i: (t0 + i, 0)),
            pl.BlockSpec((1, 1, d), lambda i: (nidx, 0, 0)),
            pl.BlockSpec((1, 1, d), lambda i: (nidx, 0, 0)),
            pl.BlockSpec((1, d, d), lambda i: (j, 0, 0)),
            pl.BlockSpec((1, ng, gw, gw), lambda i: (j, 0, 0, 0)),
            pl.BlockSpec((1, 1, d), lambda i: (j, 0, 0)),
            pl.BlockSpec((1, 1, d), lambda i: (j, 0, 0)),
            pl.BlockSpec((1, 1, HIST_ROWS, d), lambda i: (j, i // tpb, 0, 0)),
        ],
        out_specs=[
            pl.BlockSpec((tt, d), lambda i: (t0 + i, 0)),
            pl.BlockSpec((1, POOL_HIST, d), lambda i: (i // tpb, 0, 0)),
        ],
        out_shape=[
            jax.ShapeDtypeStruct(x.shape, x.dtype),
            jax.ShapeDtypeStruct((nb, POOL_HIST, d), F32),
        ],
        scratch_shapes=[pltpu.VMEM((HIST_ROWS + tt, d), F32)],
        input_output_aliases={0: 0},
        compiler_params=_params(),
        name="pool_mixer",
    )(x, npre, npost, win, wgrp, bgrp, scale, hist)


def _cmul(ar, ai, br, bi):
    return ar * br - ai * bi, ar * bi + ai * br


def _s5_param_body(are_r, aim_r, are_c, aim_c, ldt_ref, btr_ref, bti_ref, cr_ref, ci_ref,
                   ctr_ref, cti_ref, kmat_ref, wst_ref, cpow_ref, lam_ref, *, ch, sample_len):
    nc, p = btr_ref.shape[1], btr_ref.shape[2]
    dt = jnp.exp(ldt_ref[0])

    lre, lim = are_r[0], aim_r[0]
    mag = jnp.exp(lre * dt)
    lbr, lbi = mag * jnp.cos(lim * dt), mag * jnp.sin(lim * dt)
    den = lre * lre + lim * lim
    qr = ((lbr - 1.0) * lre + lbi * lim) / den
    qi = (lbi * lre - (lbr - 1.0) * lim) / den
    bbr, bbi = _cmul(qr, qi, btr_ref[0], bti_ref[0])

    def pow_rows(expo):
        m = jnp.exp(expo * (lre * dt))
        a = expo * (lim * dt)
        return m * jnp.cos(a), m * jnp.sin(a)

    s_idx = lax.broadcasted_iota(jnp.int32, (ch, 1), 0)
    rev_r, rev_i = pow_rows((ch - 1 - s_idx).astype(F32))
    srev_r, srev_i = pow_rows(jnp.maximum(sample_len - 1 - s_idx, 0).astype(F32))
    live = s_idx < sample_len
    srev_r = jnp.where(live, srev_r, 0.0)
    srev_i = jnp.where(live, srev_i, 0.0)
    for c in range(nc):
        br, bi = bbr[c:c + 1, :], bbi[c:c + 1, :]
        pr, pi = _cmul(rev_r, rev_i, br, bi)
        sr, si = _cmul(srev_r, srev_i, br, bi)
        wst_ref[0, c * ch:(c + 1) * ch, :] = jnp.concatenate([pr, pi, sr, si], axis=1).astype(BF16)

    one = jnp.ones((1, 1), F32)
    llr, lli = pow_rows(one * ch)
    lsr, lsi = pow_rows(one * sample_len)
    lam_ref[0] = jnp.concatenate([llr, lli, lsr, lsi], axis=0)

    cre, cim = are_c[0], aim_c[0]
    t_idx = lax.broadcasted_iota(jnp.int32, (1, ch), 1).astype(F32)
    m0 = jnp.exp(t_idx * (cre * dt))
    a0 = t_idx * (cim * dt)
    ptr, pti = m0 * jnp.cos(a0), m0 * jnp.sin(a0)
    m1 = jnp.exp(cre * dt)
    p1r, p1i = _cmul(ptr, pti, m1 * jnp.cos(cim * dt), m1 * jnp.sin(cim * dt))

    ctr, cti = ctr_ref[0], cti_ref[0]
    for c in range(nc):
        er, ei = _cmul(ctr[:, c:c + 1], cti[:, c:c + 1], p1r, p1i)
        cpow_ref[0, 0, :, c * ch:(c + 1) * ch] = er.astype(BF16)
        cpow_ref[0, 1, :, c * ch:(c + 1) * ch] = (-ei).astype(BF16)

    cr, ci = cr_ref[0], ci_ref[0]
    wr, wi = [], []
    for c in range(nc):
        r_, i_ = _cmul(cr, ci, bbr[c:c + 1, :], bbi[c:c + 1, :])
        wr.append(r_)
        wi.append(i_)
    wr = jnp.concatenate(wr, axis=0)
    wi = jnp.concatenate(wi, axis=0)
    kmat_ref[0] = _dot_hi(wr, ptr) - _dot_hi(wi, pti)


def _s5_params(a_re, a_im, log_dt, b_re, b_im, c_re, c_im, ch, sample_len):
    na, g, p, nc = b_re.shape
    n = na * g
    rows = lambda a: a.reshape(n, 1, p)
    cols = lambda a: a.reshape(n, p, 1)
    bt = lambda a: jnp.swapaxes(a, 2, 3).reshape(n, nc, p)
    ct = lambda a: jnp.swapaxes(a, 2, 3).reshape(n, p, nc)
    spec = lambda *shape: pl.BlockSpec((1,) + shape, lambda i: (i,) + (0,) * len(shape))
    return pl.pallas_call(
        functools.partial(_s5_param_body, ch=ch, sample_len=sample_len),
        grid=(n,),
        in_specs=[spec(1, p), spec(1, p), spec(p, 1), spec(p, 1), spec(1, 1),
                  spec(nc, p), spec(nc, p), spec(nc, p), spec(nc, p), spec(p, nc), spec(p, nc)],
        out_specs=[spec(nc * nc, ch), spec(nc * ch, 4 * p), spec(2, p, nc * ch), spec(4, p)],
        out_shape=[
            jax.ShapeDtypeStruct((n, nc * nc, ch), F32),
            jax.ShapeDtypeStruct((n, nc * ch, 4 * p), BF16),
            jax.ShapeDtypeStruct((n, 2, p, nc * ch), BF16),
            jax.ShapeDtypeStruct((n, 4, p), F32),
        ],
        compiler_params=_params(),
        name="s5_params",
    )(rows(a_re), rows(a_im), cols(a_re), cols(a_im), log_dt.reshape(n, 1, 1),
      bt(b_re), bt(b_im), c_re.reshape(n, nc, p), c_im.reshape(n, nc, p), ct(c_re), ct(c_im))


def _s5_in_body(x_ref, gpre_ref, wt_ref, u_ref):
    tt = x_ref.shape[0]
    hn = _rms(x_ref[...], gpre_ref[0]).astype(BF16)
    ut = lax.dot_general(wt_ref[0], hn, (((1,), (1,)), ((), ())), preferred_element_type=F32)
    dm = ut.shape[0]
    for kk in range(tt // S5_CHUNK):
        u_ref[:, kk, :, :] = ut[:, kk * S5_CHUNK:(kk + 1) * S5_CHUNK].reshape(
            dm // SUBLANES, SUBLANES, S5_CHUNK)


def _s5_in(x, npre, wt, layer, j, tt):
    nt, d = x.shape
    dm = wt.shape[1]
    nidx = layer * 4 + 1
    kpt = tt // S5_CHUNK
    return pl.pallas_call(
        _s5_in_body,
        grid=(nt // tt,),
        in_specs=[
            pl.BlockSpec((tt, d), lambda i: (i, 0)),
            pl.BlockSpec((1, 1, d), lambda i: (nidx, 0, 0)),
            pl.BlockSpec((1, dm, d), lambda i: (j, 0, 0)),
        ],
        out_specs=pl.BlockSpec((dm // SUBLANES, kpt, SUBLANES, S5_CHUNK), lambda i: (0, i, 0, 0)),
        out_shape=jax.ShapeDtypeStruct((dm // SUBLANES, nt // S5_CHUNK, SUBLANES, S5_CHUNK), F32),
        compiler_params=_params(),
        name="s5_in",
    )(x, npre, wt)


def _s5_scan_body(u_ref, kmat_ref, wst_ref, cpow_ref, lam_ref, dl_ref, h0r_ref, h0i_ref,
                  z_ref, hpr_ref, hpi_ref, hsr_ref, hsi_ref, t_ref, *, nc, n_prompt_rows,
                  rows_per_stream, n_streams):
    ch = S5_CHUNK
    nrows = u_ref.shape[0] // nc
    p = lam_ref.shape[2]
    npr = n_prompt_rows
    rps = rows_per_stream

    row_i = lax.broadcasted_iota(jnp.int32, (ch, ch), 0)
    col_i = lax.broadcasted_iota(jnp.int32, (ch, ch), 1)
    causal = col_i >= row_i

    def build(cp, carry):
        kr = kmat_ref[0, pl.ds(pl.multiple_of(cp * nc, nc), nc), :]
        for c in range(nc):
            blk = jnp.broadcast_to(kr[c:c + 1, :], (ch, ch))
            blk = pltpu.roll(blk, 0, 1, stride=1, stride_axis=0)
            t_ref[pl.ds(pl.multiple_of(cp * ch, ch), ch), c * ch:(c + 1) * ch] = (
                jnp.where(causal, blk, 0.0).astype(BF16))
        return carry

    lax.fori_loop(0, nc, build, 0)

    pieces = []
    for cp in range(nc):
        start = (cp // SUBLANES) * nrows * SUBLANES + (cp % SUBLANES)
        pieces.append(u_ref[pl.ds(start, nrows, stride=SUBLANES), :])
    lhs = jnp.concatenate(pieces, axis=1)
    lhs_b = lhs.astype(BF16)

    y = _dot(lhs_b, t_ref[...])
    hl = _dot(lhs_b, wst_ref[0])

    lam = lam_ref[0]
    hr, hi = hl[0:npr, 0:p], hl[0:npr, p:2 * p]
    k_in_stream = lax.broadcasted_iota(jnp.int32, (npr, 1), 0) % rps
    ar, ai = lam[0:1, :], lam[1:2, :]
    dist = 1
    while dist < rps:
        sr = jnp.where(k_in_stream >= dist, pltpu.roll(hr, dist, 0), 0.0)
        si = jnp.where(k_in_stream >= dist, pltpu.roll(hi, dist, 0), 0.0)
        mr, mi = _cmul(ar, ai, sr, si)
        hr, hi = hr + mr, hi + mi
        ar, ai = _cmul(ar, ai, ar, ai)
        dist *= 2
    for b in range(n_streams):
        last = (b + 1) * rps - 1
        hpr_ref[0, b:b + 1, :] = hr[last:last + 1, :]
        hpi_ref[0, b:b + 1, :] = hi[last:last + 1, :]
    ser = jnp.where(k_in_stream >= 1, pltpu.roll(hr, 1, 0), 0.0)
    sei = jnp.where(k_in_stream >= 1, pltpu.roll(hi, 1, 0), 0.0)

    h0r, h0i = h0r_ref[0], h0i_ref[0]
    fr, fi = _cmul(lam[2:3, :], lam[3:4, :], h0r, h0i)
    hsr_ref[0] = fr + hl[npr:, 2 * p:3 * p]
    hsi_ref[0] = fi + hl[npr:, 3 * p:4 * p]

    sr_all = jnp.concatenate([ser, h0r], axis=0).astype(BF16)
    si_all = jnp.concatenate([sei, h0i], axis=0).astype(BF16)
    y = y + _dot(sr_all, cpow_ref[0, 0]) + _dot(si_all, cpow_ref[0, 1])

    z = jax.nn.gelu(y + dl_ref[0] * lhs)
    for c in range(nc):
        start = (c // SUBLANES) * nrows * SUBLANES + (c % SUBLANES)
        z_ref[pl.ds(start, nrows, stride=SUBLANES), :] = z[:, c * ch:(c + 1) * ch]


def _s5_scan(u2d, kmat, wst, cpow, lam, dl, h0r, h0i, j, g, nc, n_prompt_rows, rows_per_stream,
             n_streams):
    ch = S5_CHUNK
    nrows = u2d.shape[0] // (g * nc)
    p = lam.shape[2]
    nsr = nrows - n_prompt_rows
    blk = nrows * nc
    body = functools.partial(_s5_scan_body, nc=nc, n_prompt_rows=n_prompt_rows,
                             rows_per_stream=rows_per_stream, n_streams=n_streams)
    gi = lambda i: (j * g + i, 0, 0)
    return pl.pallas_call(
        body,
        grid=(g,),
        in_specs=[
            pl.BlockSpec((blk, ch), lambda i: (i, 0)),
            pl.BlockSpec((1, nc * nc, ch), gi),
            pl.BlockSpec((1, nc * ch, 4 * p), gi),
            pl.BlockSpec((1, 2, p, nc * ch), lambda i: (j * g + i, 0, 0, 0)),
            pl.BlockSpec((1, 4, p), gi),
            pl.BlockSpec((1, 1, nc * ch), gi),
            pl.BlockSpec((1, nsr, p), gi),
            pl.BlockSpec((1, nsr, p), gi),
        ],
        out_specs=[
            pl.BlockSpec((blk, ch), lambda i: (i, 0)),
            pl.BlockSpec((1, n_streams, p), lambda i: (i, 0, 0)),
            pl.BlockSpec((1, n_streams, p), lambda i: (i, 0, 0)),
            pl.BlockSpec((1, nsr, p), lambda i: (i, 0, 0)),
            pl.BlockSpec((1, nsr, p), lambda i: (i, 0, 0)),
        ],
        out_shape=[
            jax.ShapeDtypeStruct(u2d.shape, F32),
            jax.ShapeDtypeStruct((g, n_streams, p), F32),
            jax.ShapeDtypeStruct((g, n_streams, p), F32),
            jax.ShapeDtypeStruct((g, nsr, p), F32),
            jax.ShapeDtypeStruct((g, nsr, p), F32),
        ],
        scratch_shapes=[pltpu.VMEM((nc * ch, nc * ch), BF16)],
        compiler_params=_params(),
        name="s5_scan",
    )(u2d, kmat, wst, cpow, lam, dl, h0r, h0i)


def _s5_out_body(x_ref, z_ref, gpost_ref, wglu_ref, o_ref):
    d = x_ref.shape[1]
    dm = z_ref.shape[0] * SUBLANES
    zt = jnp.concatenate(
        [z_ref[:, kk, :, :].reshape(dm, S5_CHUNK) for kk in range(z_ref.shape[1])], axis=1)
    gate = lax.dot_general(zt.astype(BF16), wglu_ref[0], (((0,), (0,)), ((), ())),
                           preferred_element_type=F32)
    out = gate[:, :d] * jax.nn.sigmoid(gate[:, d:])
    o_ref[...] = x_ref[...] + _rms(out, gpost_ref[0])


def _s5_out(x, z4d, npost, wglu, layer, j, tt):
    nt, d = x.shape
    dm = wglu.shape[1]
    nidx = layer * 4 + 1
    kpt = tt // S5_CHUNK
    return pl.pallas_call(
        _s5_out_body,
        grid=(nt // tt,),
        in_specs=[
            pl.BlockSpec((tt, d), lambda i: (i, 0)),
            pl.BlockSpec((dm // SUBLANES, kpt, SUBLANES, S5_CHUNK), lambda i: (0, i, 0, 0)),
            pl.BlockSpec((1, 1, d), lambda i: (nidx, 0, 0)),
            pl.BlockSpec((1, dm, 2 * d), lambda i: (j, 0, 0)),
        ],
        out_specs=pl.BlockSpec((tt, d), lambda i: (i, 0)),
        out_shape=jax.ShapeDtypeStruct(x.shape, x.dtype),
        compiler_params=_params(),
        name="s5_out",
    )(x, z4d, npost, wglu)


def kernel(x_prompt, x_sample, cache_mem_k, cache_mem_v, state_ssm_re, state_ssm_im, state_pool,
           mem_prompt, norm_pre, norm_post, norm_mem, ffn_w_gate, ffn_w_up, ffn_w_down,
           xa_w_q, xa_w_k, xa_w_v, xa_w_o,
           s5_w_in, s5_a_re, s5_a_im, s5_b_re, s5_b_im, s5_c_re, s5_c_im, s5_d, s5_log_dt, s5_w_glu,
           pool_w_in, pool_w_grp, pool_b_grp, pool_scale):
    b, seq, d = x_prompt.shape
    db, ds, _ = x_sample.shape
    depth, nsub, _ = norm_pre.shape
    nm, nh = cache_mem_k.shape[2], cache_mem_k.shape[3]
    na, g, p, nc = s5_b_re.shape
    nb_layers = pool_w_in.shape[0]
    f = ffn_w_gate.shape[-1]
    ch = S5_CHUNK
    assert seq % ch == 0 and ds <= ch and ds >= POOL_HIST and PAST_LEN >= POOL_HIST
    assert nc % SUBLANES == 0 and g * nc == d

    n_prompt = b * seq
    nt = n_prompt + db * ch
    tm = min(512, seq)
    assert seq % tm == 0 and tm % ch == 0 and nt % tm == 0

    wg = ffn_w_gate.astype(BF16).reshape(depth * 2, d, f)
    wu = ffn_w_up.astype(BF16).reshape(depth * 2, d, f)
    wd = ffn_w_down.astype(BF16).reshape(depth * 2, f, d)
    wq, wk = xa_w_q.astype(BF16), xa_w_k.astype(BF16)
    wv, wo = xa_w_v.astype(BF16), xa_w_o.astype(BF16)
    s5_wt = jnp.swapaxes(s5_w_in, 1, 2).astype(BF16)
    s5_glu = s5_w_glu.astype(BF16)
    p_win, p_wgrp = pool_w_in.astype(BF16), pool_w_grp.astype(BF16)
    npre = norm_pre.reshape(depth * nsub, 1, d)
    npost = norm_post.reshape(depth * nsub, 1, d)

    x = jnp.concatenate([
        x_prompt.reshape(n_prompt, d),
        jnp.pad(x_sample, ((0, 0), (0, ch - ds), (0, 0))).reshape(db * ch, d)], axis=0)

    mem_k_p, mem_v_p = _mem_kv(mem_prompt, norm_mem.reshape(depth, 1, d), wk, wv)
    k_s = cache_mem_k.reshape(depth, db, nm, d)
    v_s = cache_mem_v.reshape(depth, db, nm, d)

    kmat, wst, cpow, lam = _s5_params(s5_a_re, s5_a_im, s5_log_dt, s5_b_re, s5_b_im,
                                      s5_c_re, s5_c_im, ch, ds)
    dl = jnp.repeat(s5_d.reshape(na * g, 1, nc), ch, axis=2)
    h0r = jnp.swapaxes(state_ssm_re, 1, 2).reshape(na * g, db, p)
    h0i = jnp.swapaxes(state_ssm_im, 1, 2).reshape(na * g, db, p)

    hist_p = jnp.zeros((nb_layers, b, HIST_ROWS, d), F32)
    hist_s = jnp.pad(state_pool, ((0, 0), (0, 0), (HIST_ROWS - POOL_HIST, 0), (0, 0)))
    pb = jnp.reshape(pool_b_grp, (nb_layers, 1, d))
    ps = jnp.reshape(pool_scale, (nb_layers, 1, d))

    ssm_p_re, ssm_p_im, ssm_s_re, ssm_s_im, buf_p, buf_s = [], [], [], [], [], []
    for i in range(depth):
        j = i // 2
        x = _ffn(x, npre, npost, wg, wu, wd, i, 0, tm)
        if i % 2 == 0:
            u4 = _s5_in(x, npre, s5_wt, i, j, tm)
            z2, hpr, hpi, hsr, hsi = _s5_scan(
                u4.reshape(-1, ch), kmat, wst, cpow, lam, dl, h0r, h0i, j, g, nc,
                n_prompt // ch, seq // ch, b)
            x = _s5_out(x, z2.reshape(u4.shape), npost, s5_glu, i, j, tm)
            ssm_p_re.append(jnp.swapaxes(hpr, 0, 1))
            ssm_p_im.append(jnp.swapaxes(hpi, 0, 1))
            ssm_s_re.append(jnp.swapaxes(hsr, 0, 1))
            ssm_s_im.append(jnp.swapaxes(hsi, 0, 1))
        else:
            x, bp = _pool(x, npre, npost, p_win, p_wgrp, pb, ps, hist_p, i, j, 0, seq, tm, 0, tm)
            x, bs = _pool(x, npre, npost, p_win, p_wgrp, pb, ps, hist_s, i, j, n_prompt, ch, ds,
                          PAST_LEN, ch)
            buf_p.append(bp)
            buf_s.append(bs)
        x = _attn(x, npre, npost, wq, wo, mem_k_p, mem_v_p, i, nh, 0, seq, tm)
        x = _attn(x, npre, npost, wq, wo, k_s, v_s, i, nh, n_prompt, ch, ch)
        x = _ffn(x, npre, npost, wg, wu, wd, i, 1, tm)

    y_prompt = x[:n_prompt].reshape(b, seq, d)
    y_sample = x[n_prompt:].reshape(db, ch, d)[:, :ds, :]
    hd = d // nh
    return (y_prompt, y_sample,
            mem_k_p.reshape(depth, b, nm, nh, hd), mem_v_p.reshape(depth, b, nm, nh, hd),
            jnp.stack(ssm_p_re), jnp.stack(ssm_p_im), jnp.stack(buf_p),
            jnp.stack(ssm_s_re), jnp.stack(ssm_s_im), jnp.stack(buf_s))
```

```python
import functools

import jax
import jax.numpy as jnp
from jax import lax
from jax.experimental import pallas as pl
from jax.experimental.pallas import tpu as pltpu

F32 = jnp.float32
BF16 = jnp.bfloat16

RMS_EPS = 1e-6
PAST_LEN = 1024
POOL_WINDOWS = (2, 4, 8, 16)
POOL_HIST = max(POOL_WINDOWS) - 1
HIST_ROWS = 16

LANES = 128
SUBLANES = 8
S5_CHUNK = LANES // 2
S5_GROUPS_PER_STEP = 2
VMEM_LIMIT = 56 * 1024 * 1024

_NT = (((1,), (1,)), ((), ()))
_TN = (((0,), (0,)), ((), ()))


def _params(n_axes=1, vmem=VMEM_LIMIT):
    return pltpu.CompilerParams(
        dimension_semantics=("arbitrary",) * n_axes, vmem_limit_bytes=vmem)


def _rms(x, g):
    ms = jnp.mean(x * x, axis=-1, keepdims=True)
    return x * lax.rsqrt(ms + RMS_EPS) * g


def _dot(a, b):
    return jnp.dot(a, b, preferred_element_type=F32)


def _dot_nt(a, b, precision=None):
    return lax.dot_general(a, b, _NT, preferred_element_type=F32, precision=precision)


def _ffn_body(x_ref, gpre_ref, gpost_ref, wg_ref, wu_ref, wd_ref, o_ref):
    x = x_ref[...]
    hn = _rms(x, gpre_ref[0]).astype(BF16)
    g = _dot(hn, wg_ref[0])
    u = _dot(hn, wu_ref[0])
    a = (g * jax.nn.sigmoid(g) * u).astype(BF16)
    f = _dot(a, wd_ref[0])
    o_ref[...] = x + 0.5 * _rms(f, gpost_ref[0])


def _ffn(x, npre, npost, wg, wu, wd, layer, which, tm):
    nt, d = x.shape
    f = wg.shape[-1]
    sub = 0 if which == 0 else 3
    nsub = npre.shape[0] // (wg.shape[0] // 2)
    widx = layer * 2 + which
    nidx = layer * nsub + sub
    return pl.pallas_call(
        _ffn_body,
        grid=(nt // tm,),
        in_specs=[
            pl.BlockSpec((tm, d), lambda i: (i, 0)),
            pl.BlockSpec((1, 1, d), lambda i: (nidx, 0, 0)),
            pl.BlockSpec((1, 1, d), lambda i: (nidx, 0, 0)),
            pl.BlockSpec((1, d, f), lambda i: (widx, 0, 0)),
            pl.BlockSpec((1, d, f), lambda i: (widx, 0, 0)),
            pl.BlockSpec((1, f, d), lambda i: (widx, 0, 0)),
        ],
        out_specs=pl.BlockSpec((tm, d), lambda i: (i, 0)),
        out_shape=jax.ShapeDtypeStruct(x.shape, x.dtype),
        compiler_params=_params(),
        name="ffn",
    )(x, npre, npost, wg, wu, wd)


def _kv_body(m_ref, g_ref, wk_ref, wv_ref, k_ref, v_ref):
    m = _rms(m_ref[0], g_ref[0]).astype(BF16)
    k_ref[0, 0] = _dot(m, wk_ref[0])
    v_ref[0, 0] = _dot(m, wv_ref[0])


def _mem_kv(mem, nmem, wk, wv):
    b, nm, d = mem.shape
    depth = wk.shape[0]
    out = jax.ShapeDtypeStruct((depth, b, nm, d), F32)
    return pl.pallas_call(
        _kv_body,
        grid=(depth, b),
        in_specs=[
            pl.BlockSpec((1, nm, d), lambda l, j: (j, 0, 0)),
            pl.BlockSpec((1, 1, d), lambda l, j: (l, 0, 0)),
            pl.BlockSpec((1, d, d), lambda l, j: (l, 0, 0)),
            pl.BlockSpec((1, d, d), lambda l, j: (l, 0, 0)),
        ],
        out_specs=[pl.BlockSpec((1, 1, nm, d), lambda l, j: (l, j, 0, 0))] * 2,
        out_shape=[out, out],
        compiler_params=_params(2),
        name="mem_kv",
    )(mem, nmem, wk, wv)


def _attn_body(x_ref, gpre_ref, gpost_ref, wq_ref, wo_ref, k_ref, v_ref, o_ref, *, nh):
    x = x_ref[...]
    d = x.shape[-1]
    hd = d // nh
    hn = _rms(x, gpre_ref[0]).astype(BF16)
    q = (_dot(hn, wq_ref[0]) * (hd ** -0.5)).astype(BF16)
    heads = []
    for h in range(nh):
        sl = slice(h * hd, (h + 1) * hd)
        kh = k_ref[0, 0, :, sl].astype(BF16)
        vh = v_ref[0, 0, :, sl].astype(BF16)
        s = _dot_nt(q[:, sl], kh)
        e = jnp.exp(s - jnp.max(s, axis=-1, keepdims=True))
        p = e / jnp.sum(e, axis=-1, keepdims=True)
        heads.append(_dot(p.astype(BF16), vh))
    o = jnp.concatenate(heads, axis=-1).astype(BF16)
    o_ref[...] = x + _rms(_dot(o, wo_ref[0]), gpost_ref[0])


def _attn(x, npre, npost, wq, wo, k, v, layer, nh, row0, rows_per_batch, ta):
    nt, d = x.shape
    depth, nb, nm, _ = k.shape
    nsub = npre.shape[0] // depth
    nidx = layer * nsub + 2
    t0 = row0 // ta
    tpb = rows_per_batch // ta
    return pl.pallas_call(
        functools.partial(_attn_body, nh=nh),
        grid=(nb * tpb,),
        in_specs=[
            pl.BlockSpec((ta, d), lambda i: (t0 + i, 0)),
            pl.BlockSpec((1, 1, d), lambda i: (nidx, 0, 0)),
            pl.BlockSpec((1, 1, d), lambda i: (nidx, 0, 0)),
            pl.BlockSpec((1, d, d), lambda i: (layer, 0, 0)),
            pl.BlockSpec((1, d, d), lambda i: (layer, 0, 0)),
            pl.BlockSpec((1, 1, nm, d), lambda i: (layer, i // tpb, 0, 0)),
            pl.BlockSpec((1, 1, nm, d), lambda i: (layer, i // tpb, 0, 0)),
        ],
        out_specs=pl.BlockSpec((ta, d), lambda i: (t0 + i, 0)),
        out_shape=jax.ShapeDtypeStruct(x.shape, x.dtype),
        input_output_aliases={0: 0},
        compiler_params=_params(),
        name="xattn",
    )(x, npre, npost, wq, wo, k, v)


def _pool_body(x_ref, gpre_ref, gpost_ref, win_ref, wgrp_ref, bgrp_ref, scale_ref, hist_ref,
               o_ref, buf_ref, ext_ref, *, tiles_per_batch, pos0, valid_rows):
    i = pl.program_id(0)
    tt, d = x_ref.shape
    ng = len(POOL_WINDOWS)
    gw = d // ng
    first = (i % tiles_per_batch) == 0

    x = x_ref[...]
    hn = _rms(x, gpre_ref[0]).astype(BF16)
    u = _dot(hn, win_ref[0])

    @pl.when(first)
    def _():
        ext_ref[0:HIST_ROWS, :] = hist_ref[0, 0]

    @pl.when(jnp.logical_not(first))
    def _():
        ext_ref[0:HIST_ROWS, :] = ext_ref[tt:tt + HIST_ROWS, :]

    ext_ref[HIST_ROWS:HIST_ROWS + tt, :] = u

    tile_pos = pos0 + (i % tiles_per_batch) * tt
    pos = tile_pos + lax.broadcasted_iota(jnp.int32, (tt, 1), 0)
    level = ext_ref[...]
    parts = []
    for g, w in enumerate(POOL_WINDOWS):
        level = level + pltpu.roll(level, w // 2, 0)
        cnt = jnp.minimum(pos + 1, w).astype(F32)
        parts.append(level[HIST_ROWS:HIST_ROWS + tt, 0:gw] * (1.0 / cnt))
        if g + 1 < ng:
            level = level[:, gw:]
    pooled = (jnp.concatenate(parts, axis=-1) - u).astype(BF16)

    mixed = []
    for g in range(ng):
        sl = slice(g * gw, (g + 1) * gw)
        mixed.append(_dot(pooled[:, sl], wgrp_ref[0, g]) + bgrp_ref[0, :, sl])
    out = jnp.concatenate(mixed, axis=-1) * scale_ref[0]
    o_ref[...] = x + _rms(out, gpost_ref[0])

    @pl.when((i % tiles_per_batch) == tiles_per_batch - 1)
    def _():
        buf_ref[0] = u[valid_rows - POOL_HIST:valid_rows, :]


def _pool(x, npre, npost, win, wgrp, bgrp, scale, hist, layer, j, row0, rows_per_batch,
          valid_rows_last, pos0, tt):
    nt, d = x.shape
    nb = hist.shape[1]
    nidx = layer * 4 + 1
    ng = len(POOL_WINDOWS)
    gw = d // ng
    t0 = row0 // tt
    tpb = rows_per_batch // tt
    body = functools.partial(_pool_body, tiles_per_batch=tpb, pos0=pos0, valid_rows=valid_rows_last)
    return pl.pallas_call(
        body,
        grid=(nb * tpb,),
        in_specs=[
            pl.BlockSpec((tt, d), lambda i: (t0 + i, 0)),
            pl.BlockSpec((1, 1, d), lambda i: (nidx, 0, 0)),
            pl.BlockSpec((1, 1, d), lambda i: (nidx, 0, 0)),
            pl.BlockSpec((1, d, d), lambda i: (j, 0, 0)),
            pl.BlockSpec((1, ng, gw, gw), lambda i: (j, 0, 0, 0)),
            pl.BlockSpec((1, 1, d), lambda i: (j, 0, 0)),
            pl.BlockSpec((1, 1, d), lambda i: (j, 0, 0)),
            pl.BlockSpec((1, 1, HIST_ROWS, d), lambda i: (j, i // tpb, 0, 0)),
        ],
        out_specs=[
            pl.BlockSpec((tt, d), lambda i: (t0 + i, 0)),
            pl.BlockSpec((1, POOL_HIST, d), lambda i: (i // tpb, 0, 0)),
        ],
        out_shape=[
            jax.ShapeDtypeStruct(x.shape, x.dtype),
            jax.ShapeDtypeStruct((nb, POOL_HIST, d), F32),
        ],
        scratch_shapes=[pltpu.VMEM((HIST_ROWS + tt, d), F32)],
        input_output_aliases={0: 0},
        compiler_params=_params(),
        name="pool_mixer",
    )(x, npre, npost, win, wgrp, bgrp, scale, hist)


def _cmul(ar, ai, br, bi):
    return ar * br - ai * bi, ar * bi + ai * br


def _cpow(zr, zi, n):
    rr, ri = None, None
    while n:
        if n & 1:
            rr, ri = (zr, zi) if rr is None else _cmul(rr, ri, zr, zi)
        n >>= 1
        if n:
            zr, zi = _cmul(zr, zi, zr, zi)
    return rr, ri


def _slot_channel(slot, nc):
    return slot // 2 + (nc // 2) * (slot % 2)


def _s5_param_body(are_ref, aim_ref, ldt_ref, btr_ref, bti_ref, cr_ref, ci_ref,
                   kmat_ref, wst_ref, cpow_ref, lam_ref, *, sample_len):
    ch = S5_CHUNK
    nc, p = btr_ref.shape[1], btr_ref.shape[2]
    nq = nc // 2
    dt = jnp.exp(ldt_ref[0])
    lre, lim = are_ref[0], aim_ref[0]
    zr, zi = lre * dt, lim * dt
    mag = jnp.exp(zr)
    lbr, lbi = mag * jnp.cos(zi), mag * jnp.sin(zi)
    den = lre * lre + lim * lim
    qr = ((lbr - 1.0) * lre + lbi * lim) / den
    qi = (lbi * lre - (lbr - 1.0) * lim) / den
    bbr, bbi = _cmul(qr, qi, btr_ref[0], bti_ref[0])

    n8 = lax.broadcasted_iota(jnp.int32, (SUBLANES, 1), 0)

    def base(expo):
        m = jnp.exp(expo * zr)
        a = expo * zi
        return m * jnp.cos(a), m * jnp.sin(a)

    asc_r, asc_i = base(n8.astype(F32))
    rev_r, rev_i = base((SUBLANES - 1 - n8).astype(F32))
    rows = SUBLANES
    while rows < ch:
        sr, si = _cpow(lbr, lbi, rows)
        ur, ui = _cmul(asc_r, asc_i, sr, si)
        asc_r, asc_i = jnp.concatenate([asc_r, ur], axis=0), jnp.concatenate([asc_i, ui], axis=0)
        ur, ui = _cmul(rev_r, rev_i, sr, si)
        rev_r, rev_i = jnp.concatenate([ur, rev_r], axis=0), jnp.concatenate([ui, rev_i], axis=0)
        rows *= 2
    p1r, p1i = _cmul(asc_r, asc_i, lbr, lbi)

    dead = jnp.zeros((ch - sample_len, p), F32)
    rev2_r = jnp.concatenate(
        [rev_r, jnp.concatenate([rev_r[ch - sample_len:], dead], axis=0)], axis=1)
    rev2_i = jnp.concatenate(
        [rev_i, jnp.concatenate([rev_i[ch - sample_len:], dead], axis=0)], axis=1)
    bb2_r = jnp.concatenate([bbr, bbr], axis=1)
    bb2_i = jnp.concatenate([bbi, bbi], axis=1)
    cr, ci = cr_ref[0], ci_ref[0]
    w_r = [[], []]
    w_i = [[], []]
    for slot in range(nc):
        c = _slot_channel(slot, nc)
        wr, wi = _cmul(rev2_r, rev2_i, bb2_r[c:c + 1, :], bb2_i[c:c + 1, :])
        wst_ref[0, slot * ch:(slot + 1) * ch, :] = jnp.concatenate([wr, wi], axis=1).astype(BF16)
        er, ei = _cmul(cr[c:c + 1, :], ci[c:c + 1, :], p1r, p1i)
        cpow_ref[0, 0, slot * ch:(slot + 1) * ch, :] = er.astype(BF16)
        cpow_ref[0, 1, slot * ch:(slot + 1) * ch, :] = (-ei).astype(BF16)
        for h in range(2):
            r_, i_ = _cmul(cr[h * nq:(h + 1) * nq, :], ci[h * nq:(h + 1) * nq, :],
                           bbr[c:c + 1, :], bbi[c:c + 1, :])
            w_r[h].append(r_)
            w_i[h].append(i_)

    hi = lax.Precision.HIGHEST
    halves = [_dot_nt(jnp.concatenate(w_r[h], axis=0), asc_r, hi)
              - _dot_nt(jnp.concatenate(w_i[h], axis=0), asc_i, hi) for h in range(2)]
    kmat_ref[0] = jnp.concatenate(halves, axis=1)

    llr, lli = _cpow(lbr, lbi, ch)
    lsr, lsi = _cpow(lbr, lbi, sample_len)
    lam = jnp.concatenate([llr, lli, lsr, lsi], axis=0)
    lam_ref[0] = jnp.concatenate([lam, lam], axis=1)


def _s5_params(a_re, a_im, log_dt, b_re, b_im, c_re, c_im, sample_len):
    na, g, p, nc = b_re.shape
    n = na * g
    ch = S5_CHUNK
    bt = lambda a: jnp.swapaxes(a, 2, 3).reshape(n, nc, p)
    spec = lambda *shape: pl.BlockSpec((1,) + shape, lambda i: (i,) + (0,) * len(shape))
    return pl.pallas_call(
        functools.partial(_s5_param_body, sample_len=sample_len),
        grid=(n,),
        in_specs=[spec(1, p), spec(1, p), spec(1, 1),
                  spec(nc, p), spec(nc, p), spec(nc, p), spec(nc, p)],
        out_specs=[spec(nc * nc // 2, LANES), spec(nc * ch, 4 * p), spec(2, nc * ch, p),
                   spec(4, 2 * p)],
        out_shape=[
            jax.ShapeDtypeStruct((n, nc * nc // 2, LANES), F32),
            jax.ShapeDtypeStruct((n, nc * ch, 4 * p), BF16),
            jax.ShapeDtypeStruct((n, 2, nc * ch, p), BF16),
            jax.ShapeDtypeStruct((n, 4, 2 * p), F32),
        ],
        compiler_params=_params(),
        name="s5_params",
    )(a_re.reshape(n, 1, p), a_im.reshape(n, 1, p), log_dt.reshape(n, 1, 1),
      bt(b_re), bt(b_im), c_re.reshape(n, nc, p), c_im.reshape(n, nc, p))


def _lane_is_low(shape):
    return lax.broadcasted_iota(jnp.int32, shape, len(shape) - 1) < S5_CHUNK


def _s5_in_body(x_ref, gpre_ref, wt_ref, u_ref, *, nc):
    tt = x_ref.shape[0]
    nq = nc // 2
    hn = _rms(x_ref[...], gpre_ref[0]).astype(BF16)
    ut = _dot_nt(wt_ref[0], hn)
    ut = ut.reshape(ut.shape[0] // nc, nc, tt)
    low = _lane_is_low((ut.shape[0], nq, LANES))
    for m in range(tt // LANES):
        a = ut[:, 0:nq, m * LANES:(m + 1) * LANES]
        b = ut[:, nq:nc, m * LANES:(m + 1) * LANES]
        u_ref[:, (2 * m) * nq:(2 * m + 1) * nq, :] = jnp.where(low, a, pltpu.roll(b, S5_CHUNK, 2))
        u_ref[:, (2 * m + 1) * nq:(2 * m + 2) * nq, :] = jnp.where(low, pltpu.roll(a, S5_CHUNK, 2), b)


def _s5_in(x, npre, wt, layer, j, nc, tt):
    nt, d = x.shape
    dm = wt.shape[1]
    g = dm // nc
    nidx = layer * 4 + 1
    rows_per_tile = tt // S5_CHUNK * (nc // 2)
    return pl.pallas_call(
        functools.partial(_s5_in_body, nc=nc),
        grid=(nt // tt,),
        in_specs=[
            pl.BlockSpec((tt, d), lambda i: (i, 0)),
            pl.BlockSpec((1, 1, d), lambda i: (nidx, 0, 0)),
            pl.BlockSpec((1, dm, d), lambda i: (j, 0, 0)),
        ],
        out_specs=pl.BlockSpec((g, rows_per_tile, LANES), lambda i: (0, i, 0)),
        out_shape=jax.ShapeDtypeStruct((g, nt // S5_CHUNK * (nc // 2), LANES), F32),
        compiler_params=_params(),
        name="s5_in",
    )(x, npre, wt)


def _s5_group(gi, u_ref, kmat_ref, wst_ref, cpow_ref, lam_ref, dl_ref, h0r_ref, h0i_ref,
              z_ref, hpr_ref, hpi_ref, hsr_ref, hsi_ref, t_ref, *, nc, n_prompt_rows,
              rows_per_stream, n_streams):
    ch = S5_CHUNK
    nq = nc // 2
    nrows = u_ref.shape[1] // nq
    p = h0r_ref.shape[2]
    npr = n_prompt_rows
    rps = rows_per_stream

    row_i = lax.broadcasted_iota(jnp.int32, (ch, LANES), 0)
    col_i = lax.broadcasted_iota(jnp.int32, (ch, LANES), 1)
    causal = (col_i % ch) >= row_i
    for slot in range(nc):
        kr = kmat_ref[gi, slot * nq:(slot + 1) * nq, :]
        for q in range(nq):
            blk = jnp.broadcast_to(kr[q:q + 1, :], (ch, LANES))
            blk = pltpu.roll(blk, 0, 1, stride=1, stride_axis=0)
            t_ref[gi, slot * ch:(slot + 1) * ch, q * LANES:(q + 1) * LANES] = (
                jnp.where(causal, blk, 0.0).astype(BF16))

    lhs = jnp.concatenate(
        [u_ref[gi, pl.ds(q, nrows, stride=nq), :] for q in range(nq)], axis=1)
    lhs_b = lhs.astype(BF16)

    y = _dot(lhs_b, t_ref[gi])
    hl = _dot(lhs_b, wst_ref[gi])
    hl_r, hl_i = hl[:, 0:2 * p], hl[:, 2 * p:4 * p]

    lam = lam_ref[gi]
    hr, hi = hl_r[0:npr], hl_i[0:npr]
    k_in_stream = lax.broadcasted_iota(jnp.int32, (npr, 1), 0) % rps
    ar, ai = lam[0:1, :], lam[1:2, :]
    dist = 1
    while dist < rps:
        sr = jnp.where(k_in_stream >= dist, pltpu.roll(hr, dist, 0), 0.0)
        si = jnp.where(k_in_stream >= dist, pltpu.roll(hi, dist, 0), 0.0)
        mr, mi = _cmul(ar, ai, sr, si)
        hr, hi = hr + mr, hi + mi
        ar, ai = _cmul(ar, ai, ar, ai)
        dist *= 2
    for b in range(n_streams):
        last = (b + 1) * rps - 1
        hpr_ref[gi, b:b + 1, :] = hr[last:last + 1, 0:p]
        hpi_ref[gi, b:b + 1, :] = hi[last:last + 1, 0:p]
    ser = jnp.where(k_in_stream >= 1, pltpu.roll(hr, 1, 0), 0.0)[:, 0:p]
    sei = jnp.where(k_in_stream >= 1, pltpu.roll(hi, 1, 0), 0.0)[:, 0:p]

    h0r, h0i = h0r_ref[gi], h0i_ref[gi]
    fr, fi = _cmul(lam[2:3, 0:p], lam[3:4, 0:p], h0r, h0i)
    hsr_ref[gi] = fr + hl_r[npr:, p:2 * p]
    hsi_ref[gi] = fi + hl_i[npr:, p:2 * p]

    sr_all = jnp.concatenate([ser, h0r], axis=0).astype(BF16)
    si_all = jnp.concatenate([sei, h0i], axis=0).astype(BF16)
    y = y + _dot_nt(sr_all, cpow_ref[gi, 0]) + _dot_nt(si_all, cpow_ref[gi, 1])

    z = jax.nn.gelu(y + dl_ref[gi] * lhs)
    for q in range(nq):
        z_ref[gi, pl.ds(q, nrows, stride=nq), :] = z[:, q * LANES:(q + 1) * LANES]


def _s5_scan_body(*refs, groups, **static):
    for gi in range(groups):
        _s5_group(gi, *refs, **static)


def _s5_scan(u, kmat, wst, cpow, lam, dl, h0r, h0i, j, nc, n_prompt_rows, rows_per_stream,
             n_streams):
    ch = S5_CHUNK
    g, blk, _ = u.shape
    gps = S5_GROUPS_PER_STEP
    assert g % gps == 0
    nrows = blk // (nc // 2)
    p = h0r.shape[2]
    nsr = nrows - n_prompt_rows
    body = functools.partial(_s5_scan_body, groups=gps, nc=nc, n_prompt_rows=n_prompt_rows,
                             rows_per_stream=rows_per_stream, n_streams=n_streams)
    j0 = j * (g // gps)
    par = lambda *shape: pl.BlockSpec((gps,) + shape, lambda i: (j0 + i,) + (0,) * len(shape))
    act = lambda *shape: pl.BlockSpec((gps,) + shape, lambda i: (i,) + (0,) * len(shape))
    return pl.pallas_call(
        body,
        grid=(g // gps,),
        in_specs=[
            act(blk, LANES),
            par(nc * nc // 2, LANES),
            par(nc * ch, 4 * p),
            par(2, nc * ch, p),
            par(4, 2 * p),
            par(1, nc * ch),
            par(nsr, p),
            par(nsr, p),
        ],
        out_specs=[act(blk, LANES), act(n_streams, p), act(n_streams, p), act(nsr, p), act(nsr, p)],
        out_shape=[
            jax.ShapeDtypeStruct(u.shape, F32),
            jax.ShapeDtypeStruct((g, n_streams, p), F32),
            jax.ShapeDtypeStruct((g, n_streams, p), F32),
            jax.ShapeDtypeStruct((g, nsr, p), F32),
            jax.ShapeDtypeStruct((g, nsr, p), F32),
        ],
        scratch_shapes=[pltpu.VMEM((gps, nc * ch, nc * ch), BF16)],
        compiler_params=_params(),
        name="s5_scan",
    )(u, kmat, wst, cpow, lam, dl, h0r, h0i)


def _s5_out_body(x_ref, z_ref, gpost_ref, wglu_ref, o_ref, *, nc):
    tt, d = x_ref.shape
    g = z_ref.shape[0]
    nq = nc // 2
    low = _lane_is_low((g, nq, LANES))
    blocks = []
    for m in range(tt // LANES):
        e = z_ref[:, (2 * m) * nq:(2 * m + 1) * nq, :]
        o = z_ref[:, (2 * m + 1) * nq:(2 * m + 2) * nq, :]
        a = jnp.where(low, e, pltpu.roll(o, S5_CHUNK, 2))
        b = jnp.where(low, pltpu.roll(e, S5_CHUNK, 2), o)
        blocks.append(jnp.concatenate([a, b], axis=1).reshape(g * nc, LANES))
    zt = jnp.concatenate(blocks, axis=1)
    gate = lax.dot_general(zt.astype(BF16), wglu_ref[0], _TN, preferred_element_type=F32)
    out = gate[:, :d] * jax.nn.sigmoid(gate[:, d:])
    o_ref[...] = x_ref[...] + _rms(out, gpost_ref[0])


def _s5_out(x, z, npost, wglu, layer, j, nc, tt):
    nt, d = x.shape
    dm = wglu.shape[1]
    g = dm // nc
    nidx = layer * 4 + 1
    rows_per_tile = tt // S5_CHUNK * (nc // 2)
    return pl.pallas_call(
        functools.partial(_s5_out_body, nc=nc),
        grid=(nt // tt,),
        in_specs=[
            pl.BlockSpec((tt, d), lambda i: (i, 0)),
            pl.BlockSpec((g, rows_per_tile, LANES), lambda i: (0, i, 0)),
            pl.BlockSpec((1, 1, d), lambda i: (nidx, 0, 0)),
            pl.BlockSpec((1, dm, 2 * d), lambda i: (j, 0, 0)),
        ],
        out_specs=pl.BlockSpec((tt, d), lambda i: (i, 0)),
        out_shape=jax.ShapeDtypeStruct(x.shape, x.dtype),
        compiler_params=_params(),
        name="s5_out",
    )(x, z, npost, wglu)


def kernel(x_prompt, x_sample, cache_mem_k, cache_mem_v, state_ssm_re, state_ssm_im, state_pool,
           mem_prompt, norm_pre, norm_post, norm_mem, ffn_w_gate, ffn_w_up, ffn_w_down,
           xa_w_q, xa_w_k, xa_w_v, xa_w_o,
           s5_w_in, s5_a_re, s5_a_im, s5_b_re, s5_b_im, s5_c_re, s5_c_im, s5_d, s5_log_dt, s5_w_glu,
           pool_w_in, pool_w_grp, pool_b_grp, pool_scale):
    b, seq, d = x_prompt.shape
    db, ds, _ = x_sample.shape
    depth, nsub, _ = norm_pre.shape
    nm, nh = cache_mem_k.shape[2], cache_mem_k.shape[3]
    na, g, p, nc = s5_b_re.shape
    nb_layers = pool_w_in.shape[0]
    f = ffn_w_gate.shape[-1]
    ch = S5_CHUNK
    assert ds <= ch and ds % SUBLANES == 0 and ds >= POOL_HIST and PAST_LEN >= POOL_HIST
    assert nc == 2 * SUBLANES and g * nc == d and 2 * p == LANES

    n_prompt = b * seq
    nt = n_prompt + db * ch
    tm = min(512, seq)
    assert seq % tm == 0 and tm % LANES == 0 and nt % tm == 0

    wg = ffn_w_gate.astype(BF16).reshape(depth * 2, d, f)
    wu = ffn_w_up.astype(BF16).reshape(depth * 2, d, f)
    wd = ffn_w_down.astype(BF16).reshape(depth * 2, f, d)
    wq, wk = xa_w_q.astype(BF16), xa_w_k.astype(BF16)
    wv, wo = xa_w_v.astype(BF16), xa_w_o.astype(BF16)
    s5_wt = jnp.swapaxes(s5_w_in, 1, 2).astype(BF16)
    s5_glu = s5_w_glu.astype(BF16)
    p_win, p_wgrp = pool_w_in.astype(BF16), pool_w_grp.astype(BF16)
    npre = norm_pre.reshape(depth * nsub, 1, d)
    npost = norm_post.reshape(depth * nsub, 1, d)

    x = jnp.concatenate([
        x_prompt.reshape(n_prompt, d),
        jnp.pad(x_sample, ((0, 0), (0, ch - ds), (0, 0))).reshape(db * ch, d)], axis=0)

    mem_k_p, mem_v_p = _mem_kv(mem_prompt, norm_mem.reshape(depth, 1, d), wk, wv)
    k_s = cache_mem_k.reshape(depth, db, nm, d)
    v_s = cache_mem_v.reshape(depth, db, nm, d)

    kmat, wst, cpow, lam = _s5_params(s5_a_re, s5_a_im, s5_log_dt, s5_b_re, s5_b_im,
                                      s5_c_re, s5_c_im, ds)
    slot_channel = [_slot_channel(slot, nc) for slot in range(nc)]
    dl = jnp.repeat(s5_d.reshape(na * g, 1, nc)[:, :, slot_channel], ch, axis=2)
    h0r = jnp.swapaxes(state_ssm_re, 1, 2).reshape(na * g, db, p)
    h0i = jnp.swapaxes(state_ssm_im, 1, 2).reshape(na * g, db, p)

    hist_p = jnp.zeros((nb_layers, b, HIST_ROWS, d), F32)
    hist_s = jnp.pad(state_pool, ((0, 0), (0, 0), (HIST_ROWS - POOL_HIST, 0), (0, 0)))
    pb = jnp.reshape(pool_b_grp, (nb_layers, 1, d))
    ps = jnp.reshape(pool_scale, (nb_layers, 1, d))

    ssm_p_re, ssm_p_im, ssm_s_re, ssm_s_im, buf_p, buf_s = [], [], [], [], [], []
    for i in range(depth):
        j = i // 2
        x = _ffn(x, npre, npost, wg, wu, wd, i, 0, tm)
        if i % 2 == 0:
            u = _s5_in(x, npre, s5_wt, i, j, nc, tm)
            z, hpr, hpi, hsr, hsi = _s5_scan(
                u, kmat, wst, cpow, lam, dl, h0r, h0i, j, nc, n_prompt // ch, seq // ch, b)
            x = _s5_out(x, z, npost, s5_glu, i, j, nc, tm)
            ssm_p_re.append(jnp.swapaxes(hpr, 0, 1))
            ssm_p_im.append(jnp.swapaxes(hpi, 0, 1))
            ssm_s_re.append(jnp.swapaxes(hsr, 0, 1))
            ssm_s_im.append(jnp.swapaxes(hsi, 0, 1))
        else:
            x, bp = _pool(x, npre, npost, p_win, p_wgrp, pb, ps, hist_p, i, j, 0, seq, tm, 0, tm)
            x, bs = _pool(x, npre, npost, p_win, p_wgrp, pb, ps, hist_s, i, j, n_prompt, ch, ds,
                          PAST_LEN, ch)
            buf_p.append(bp)
            buf_s.append(bs)
        x = _attn(x, npre, npost, wq, wo, mem_k_p, mem_v_p, i, nh, 0, seq, tm)
        x = _attn(x, npre, npost, wq, wo, k_s, v_s, i, nh, n_prompt, ch, ch)
        x = _ffn(x, npre, npost, wg, wu, wd, i, 1, tm)

    y_prompt = x[:n_prompt].reshape(b, seq, d)
    y_sample = x[n_prompt:].reshape(db, ch, d)[:, :ds, :]
    hd = d // nh
    return (y_prompt, y_sample,
            mem_k_p.reshape(depth, b, nm, nh, hd), mem_v_p.reshape(depth, b, nm, nh, hd),
            jnp.stack(ssm_p_re), jnp.stack(ssm_p_im), jnp.stack(buf_p),
            jnp.stack(ssm_s_re), jnp.stack(ssm_s_im), jnp.stack(buf_s))
```

```python
import functools

import jax
import jax.numpy as jnp
from jax import lax
from jax.experimental import pallas as pl
from jax.experimental.pallas import tpu as pltpu

F32 = jnp.float32
BF16 = jnp.bfloat16

RMS_EPS = 1e-6
PAST_LEN = 1024
POOL_WINDOWS = (2, 4, 8, 16)
POOL_HIST = max(POOL_WINDOWS) - 1
HIST_ROWS = 16

LANES = 128
SUBLANES = 8
S5_CHUNK = LANES // 2
S5_GROUPS_PER_STEP = 2
FFN_ROW_CHUNK = 256
ATTN_TILE = 1024
ATTN_ROW_CHUNK = 1024
VMEM_LIMIT = 56 * 1024 * 1024

_NT = (((1,), (1,)), ((), ()))
_TN = (((0,), (0,)), ((), ()))


def _params(n_axes=1, vmem=VMEM_LIMIT):
    return pltpu.CompilerParams(
        dimension_semantics=("arbitrary",) * n_axes, vmem_limit_bytes=vmem)


def _rms(x, g):
    ms = jnp.mean(x * x, axis=-1, keepdims=True)
    return x * lax.rsqrt(ms + RMS_EPS) * g


def _dot(a, b):
    return jnp.dot(a, b, preferred_element_type=F32)


def _dot_nt(a, b, precision=None):
    return lax.dot_general(a, b, _NT, preferred_element_type=F32, precision=precision)


def _ffn_body(*refs, n_in, n_out, n_prompt_tiles, row_chunk):
    x_refs, (gpre_ref, gpost_ref, wg_ref, wu_ref, wd_ref) = refs[:n_in], refs[n_in:n_in + 5]
    o_refs = refs[n_in + 5:]
    i = pl.program_id(0)
    tm = x_refs[0].shape[0]
    on_prompt = i < n_prompt_tiles
    for r in range(tm // row_chunk):
        rows = slice(r * row_chunk, (r + 1) * row_chunk)
        x = x_refs[0][rows, :]
        if n_in == 2:
            x = jnp.where(on_prompt, x, x_refs[1][rows, :])
        hn = _rms(x, gpre_ref[0]).astype(BF16)
        g = _dot(hn, wg_ref[0])
        u = _dot(hn, wu_ref[0])
        a = (g * jax.nn.sigmoid(g) * u).astype(BF16)
        o_refs[-1][rows, :] = x + 0.5 * _rms(_dot(a, wd_ref[0]), gpost_ref[0])
    if n_out == 2:
        @pl.when(on_prompt)
        def _():
            o_refs[0][...] = o_refs[1][...]


def _ffn(xs, npre, npost, wg, wu, wd, layer, which, tm, n_prompt_tiles, split_out=False):
    d = xs[0].shape[1]
    f = wg.shape[-1]
    sub = 0 if which == 0 else 3
    nsub = npre.shape[0] // (wg.shape[0] // 2)
    widx = layer * 2 + which
    nidx = layer * nsub + sub
    last_prompt = n_prompt_tiles - 1
    joint = pl.BlockSpec((tm, d), lambda i: (i, 0))
    prompt = pl.BlockSpec((tm, d), lambda i: (jnp.minimum(i, last_prompt), 0))
    sample = pl.BlockSpec((tm, d), lambda i: (0, 0))
    if len(xs) == 2:
        assert xs[0].shape[0] == n_prompt_tiles * tm and xs[1].shape[0] == tm
    nt = (n_prompt_tiles + 1) * tm
    if split_out:
        out_specs = [prompt, sample]
        out_shape = [jax.ShapeDtypeStruct((n_prompt_tiles * tm, d), F32),
                     jax.ShapeDtypeStruct((tm, d), F32)]
    else:
        out_specs = joint
        out_shape = jax.ShapeDtypeStruct((nt, d), F32)
    body = functools.partial(_ffn_body, n_in=len(xs), n_out=2 if split_out else 1,
                             n_prompt_tiles=n_prompt_tiles, row_chunk=FFN_ROW_CHUNK)
    return pl.pallas_call(
        body,
        grid=(nt // tm,),
        in_specs=([joint] if len(xs) == 1 else [prompt, sample]) + [
            pl.BlockSpec((1, 1, d), lambda i: (nidx, 0, 0)),
            pl.BlockSpec((1, 1, d), lambda i: (nidx, 0, 0)),
            pl.BlockSpec((1, d, f), lambda i: (widx, 0, 0)),
            pl.BlockSpec((1, d, f), lambda i: (widx, 0, 0)),
            pl.BlockSpec((1, f, d), lambda i: (widx, 0, 0)),
        ],
        out_specs=out_specs,
        out_shape=out_shape,
        compiler_params=_params(),
        name="ffn",
    )(*xs, npre, npost, wg, wu, wd)


def _kv_body(m_ref, g_ref, wk_ref, wv_ref, k_ref, v_ref, kb_ref, vb_ref):
    nh, hd = k_ref.shape[3], k_ref.shape[4]
    m = _rms(m_ref[0], g_ref[0]).astype(BF16)
    for w_ref, out_ref, bf_ref in ((wk_ref, k_ref, kb_ref), (wv_ref, v_ref, vb_ref)):
        y = _dot(m, w_ref[0])
        bf_ref[0, 0] = y.astype(BF16)
        for h in range(nh):
            out_ref[0, 0, :, h, :] = y[:, h * hd:(h + 1) * hd]


def _mem_kv(mem, nmem, wk, wv, nh):
    b, nm, d = mem.shape
    depth = wk.shape[0]
    hd = d // nh
    out = jax.ShapeDtypeStruct((depth, b, nm, nh, hd), F32)
    out_b = jax.ShapeDtypeStruct((depth, b, nm, d), BF16)
    return pl.pallas_call(
        _kv_body,
        grid=(depth, b),
        in_specs=[
            pl.BlockSpec((1, nm, d), lambda l, j: (j, 0, 0)),
            pl.BlockSpec((1, 1, d), lambda l, j: (l, 0, 0)),
            pl.BlockSpec((1, d, d), lambda l, j: (l, 0, 0)),
            pl.BlockSpec((1, d, d), lambda l, j: (l, 0, 0)),
        ],
        out_specs=[pl.BlockSpec((1, 1, nm, nh, hd), lambda l, j: (l, j, 0, 0, 0))] * 2
        + [pl.BlockSpec((1, 1, nm, d), lambda l, j: (l, j, 0, 0))] * 2,
        out_shape=[out, out, out_b, out_b],
        compiler_params=_params(2),
        name="mem_kv",
    )(mem, nmem, wk, wv)


def _attn_body(x_ref, gpre_ref, gpost_ref, wq_ref, wo_ref, k_ref, v_ref, o_ref, *, nh):
    ta, d = x_ref.shape
    hd = d // nh
    kv = []
    for h in range(nh):
        if len(k_ref.shape) == 5:
            kh, vh = k_ref[0, 0, :, h, :], v_ref[0, 0, :, h, :]
        else:
            kh, vh = k_ref[0, 0, :, h * hd:(h + 1) * hd], v_ref[0, 0, :, h * hd:(h + 1) * hd]
        kv.append((kh.astype(BF16), vh.astype(BF16)))
    row_chunk = min(ta, ATTN_ROW_CHUNK)
    for r in range(ta // row_chunk):
        rows = slice(r * row_chunk, (r + 1) * row_chunk)
        x = x_ref[rows, :]
        hn = _rms(x, gpre_ref[0]).astype(BF16)
        q = (_dot(hn, wq_ref[0]) * (hd ** -0.5)).astype(BF16)
        heads = []
        for h, (kh, vh) in enumerate(kv):
            s = _dot_nt(q[:, h * hd:(h + 1) * hd], kh)
            e = jnp.exp(s - jnp.max(s, axis=-1, keepdims=True))
            p = e / jnp.sum(e, axis=-1, keepdims=True)
            heads.append(_dot(p.astype(BF16), vh))
        o = jnp.concatenate(heads, axis=-1).astype(BF16)
        o_ref[rows, :] = x + _rms(_dot(o, wo_ref[0]), gpost_ref[0])


def _attn(x, npre, npost, wq, wo, k, v, layer, nh, row0, rows_per_batch, ta):
    nt, d = x.shape
    depth, nb = k.shape[0], k.shape[1]
    nsub = npre.shape[0] // depth
    nidx = layer * nsub + 2
    t0 = row0 // ta
    tpb = rows_per_batch // ta
    zeros = (0,) * (len(k.shape) - 2)
    kv_spec = pl.BlockSpec((1, 1) + k.shape[2:], lambda i: (layer, i // tpb) + zeros)
    return pl.pallas_call(
        functools.partial(_attn_body, nh=nh),
        grid=(nb * tpb,),
        in_specs=[
            pl.BlockSpec((ta, d), lambda i: (t0 + i, 0)),
            pl.BlockSpec((1, 1, d), lambda i: (nidx, 0, 0)),
            pl.BlockSpec((1, 1, d), lambda i: (nidx, 0, 0)),
            pl.BlockSpec((1, d, d), lambda i: (layer, 0, 0)),
            pl.BlockSpec((1, d, d), lambda i: (layer, 0, 0)),
            kv_spec,
            kv_spec,
        ],
        out_specs=pl.BlockSpec((ta, d), lambda i: (t0 + i, 0)),
        out_shape=jax.ShapeDtypeStruct(x.shape, x.dtype),
        input_output_aliases={0: 0},
        compiler_params=_params(),
        name="xattn",
    )(x, npre, npost, wq, wo, k, v)


def _pool_body(x_ref, gpre_ref, gpost_ref, win_ref, wgrp_ref, bgrp_ref, scale_ref, hist_ref,
               o_ref, buf_ref, ext_ref, *, tiles_per_batch, pos0, valid_rows):
    i = pl.program_id(0)
    tt, d = x_ref.shape
    ng = len(POOL_WINDOWS)
    gw = d // ng
    first = (i % tiles_per_batch) == 0

    x = x_ref[...]
    hn = _rms(x, gpre_ref[0]).astype(BF16)
    u = _dot(hn, win_ref[0])

    @pl.when(first)
    def _():
        ext_ref[0:HIST_ROWS, :] = hist_ref[0, 0]

    @pl.when(jnp.logical_not(first))
    def _():
        ext_ref[0:HIST_ROWS, :] = ext_ref[tt:tt + HIST_ROWS, :]

    ext_ref[HIST_ROWS:HIST_ROWS + tt, :] = u

    tile_pos = pos0 + (i % tiles_per_batch) * tt
    pos = tile_pos + lax.broadcasted_iota(jnp.int32, (tt, 1), 0)
    level = ext_ref[...]
    parts = []
    for g, w in enumerate(POOL_WINDOWS):
        level = level + pltpu.roll(level, w // 2, 0)
        cnt = jnp.minimum(pos + 1, w).astype(F32)
        parts.append(level[HIST_ROWS:HIST_ROWS + tt, 0:gw] * (1.0 / cnt))
        if g + 1 < ng:
            level = level[:, gw:]
    pooled = (jnp.concatenate(parts, axis=-1) - u).astype(BF16)

    mixed = []
    for g in range(ng):
        sl = slice(g * gw, (g + 1) * gw)
        mixed.append(_dot(pooled[:, sl], wgrp_ref[0, g]) + bgrp_ref[0, :, sl])
    out = jnp.concatenate(mixed, axis=-1) * scale_ref[0]
    o_ref[...] = x + _rms(out, gpost_ref[0])

    @pl.when((i % tiles_per_batch) == tiles_per_batch - 1)
    def _():
        buf_ref[0] = u[valid_rows - POOL_HIST:valid_rows, :]


def _pool(x, npre, npost, win, wgrp, bgrp, scale, hist, layer, j, row0, rows_per_batch,
          valid_rows_last, pos0, tt):
    nt, d = x.shape
    nb = hist.shape[1]
    nidx = layer * 4 + 1
    ng = len(POOL_WINDOWS)
    gw = d // ng
    t0 = row0 // tt
    tpb = rows_per_batch // tt
    body = functools.partial(_pool_body, tiles_per_batch=tpb, pos0=pos0, valid_rows=valid_rows_last)
    return pl.pallas_call(
        body,
        grid=(nb * tpb,),
        in_specs=[
            pl.BlockSpec((tt, d), lambda i: (t0 + i, 0)),
            pl.BlockSpec((1, 1, d), lambda i: (nidx, 0, 0)),
            pl.BlockSpec((1, 1, d), lambda i: (nidx, 0, 0)),
            pl.BlockSpec((1, d, d), lambda i: (j, 0, 0)),
            pl.BlockSpec((1, ng, gw, gw), lambda i: (j, 0, 0, 0)),
            pl.BlockSpec((1, 1, d), lambda i: (j, 0, 0)),
            pl.BlockSpec((1, 1, d), lambda i: (j, 0, 0)),
            pl.BlockSpec((1, 1, HIST_ROWS, d), lambda i: (j, i // tpb, 0, 0)),
        ],
        out_specs=[
            pl.BlockSpec((tt, d), lambda i: (t0 + i, 0)),
            pl.BlockSpec((1, POOL_HIST, d), lambda i: (i // tpb, 0, 0)),
        ],
        out_shape=[
            jax.ShapeDtypeStruct(x.shape, x.dtype),
            jax.ShapeDtypeStruct((nb, POOL_HIST, d), F32),
        ],
        scratch_shapes=[pltpu.VMEM((HIST_ROWS + tt, d), F32)],
        input_output_aliases={0: 0},
        compiler_params=_params(),
        name="pool_mixer",
    )(x, npre, npost, win, wgrp, bgrp, scale, hist)


def _cmul(ar, ai, br, bi):
    return ar * br - ai * bi, ar * bi + ai * br


def _cpow(zr, zi, n):
    rr, ri = None, None
    while n:
        if n & 1:
            rr, ri = (zr, zi) if rr is None else _cmul(rr, ri, zr, zi)
        n >>= 1
        if n:
            zr, zi = _cmul(zr, zi, zr, zi)
    return rr, ri


def _slot_channel(slot, nc):
    return slot // 2 + (nc // 2) * (slot % 2)


def _s5_param_group(gi, are_ref, aim_ref, ldt_ref, btr_ref, bti_ref, cr_ref, ci_ref,
                    kmat_ref, wst_ref, cpow_ref, lam_ref, *, sample_len):
    ch = S5_CHUNK
    nc, p = btr_ref.shape[1], btr_ref.shape[2]
    nq = nc // 2
    dt = jnp.exp(ldt_ref[gi])
    lre, lim = are_ref[gi], aim_ref[gi]
    zr, zi = lre * dt, lim * dt
    mag = jnp.exp(zr)
    lbr, lbi = mag * jnp.cos(zi), mag * jnp.sin(zi)
    den = lre * lre + lim * lim
    qr = ((lbr - 1.0) * lre + lbi * lim) / den
    qi = (lbi * lre - (lbr - 1.0) * lim) / den
    bbr, bbi = _cmul(qr, qi, btr_ref[gi], bti_ref[gi])

    n8 = lax.broadcasted_iota(jnp.int32, (SUBLANES, 1), 0)

    def base(expo):
        m = jnp.exp(expo * zr)
        a = expo * zi
        return m * jnp.cos(a), m * jnp.sin(a)

    asc_r, asc_i = base(n8.astype(F32))
    rev_r, rev_i = base((SUBLANES - 1 - n8).astype(F32))
    rows = SUBLANES
    while rows < ch:
        sr, si = _cpow(lbr, lbi, rows)
        ur, ui = _cmul(asc_r, asc_i, sr, si)
        asc_r, asc_i = jnp.concatenate([asc_r, ur], axis=0), jnp.concatenate([asc_i, ui], axis=0)
        ur, ui = _cmul(rev_r, rev_i, sr, si)
        rev_r, rev_i = jnp.concatenate([ur, rev_r], axis=0), jnp.concatenate([ui, rev_i], axis=0)
        rows *= 2
    p1r, p1i = _cmul(asc_r, asc_i, lbr, lbi)
    p1_ri = jnp.concatenate([p1r, p1i], axis=1)
    p1_ir = jnp.concatenate([p1i, p1r], axis=1)

    dead = jnp.zeros((ch - sample_len, p), F32)
    rev2_r = jnp.concatenate(
        [rev_r, jnp.concatenate([rev_r[ch - sample_len:], dead], axis=0)], axis=1)
    rev2_i = jnp.concatenate(
        [rev_i, jnp.concatenate([rev_i[ch - sample_len:], dead], axis=0)], axis=1)
    bb2_r = jnp.concatenate([bbr, bbr], axis=1)
    bb2_i = jnp.concatenate([bbi, bbi], axis=1)
    cr, ci = cr_ref[gi], ci_ref[gi]
    c_pm = jnp.concatenate([cr, -cr], axis=1)
    c_mm = jnp.concatenate([-ci, -ci], axis=1)
    w_r = [[], []]
    w_i = [[], []]
    for slot in range(nc):
        c = _slot_channel(slot, nc)
        wr, wi = _cmul(rev2_r, rev2_i, bb2_r[c:c + 1, :], bb2_i[c:c + 1, :])
        wst_ref[gi, slot * ch:(slot + 1) * ch, :] = jnp.concatenate([wr, wi], axis=1).astype(BF16)
        cpow_ref[gi, slot * ch:(slot + 1) * ch, :] = (
            p1_ri * c_pm[c:c + 1, :] + p1_ir * c_mm[c:c + 1, :]).astype(BF16)
        for h in range(2):
            r_, i_ = _cmul(cr[h * nq:(h + 1) * nq, :], ci[h * nq:(h + 1) * nq, :],
                           bbr[c:c + 1, :], bbi[c:c + 1, :])
            w_r[h].append(r_)
            w_i[h].append(i_)

    hi = lax.Precision.HIGHEST
    halves = [_dot_nt(jnp.concatenate(w_r[h], axis=0), asc_r, hi)
              - _dot_nt(jnp.concatenate(w_i[h], axis=0), asc_i, hi) for h in range(2)]
    kmat_ref[gi] = jnp.concatenate(halves, axis=1)

    llr, lli = _cpow(lbr, lbi, ch)
    lsr, lsi = _cpow(lbr, lbi, sample_len)
    lam = jnp.concatenate([llr, lli, lsr, lsi], axis=0)
    lam_ref[gi] = jnp.concatenate([lam, lam], axis=1)


def _s5_param_body(*refs, groups, sample_len):
    for gi in range(groups):
        _s5_param_group(gi, *refs, sample_len=sample_len)


def _s5_params(a_re, a_im, log_dt, b_re, b_im, c_re, c_im, sample_len):
    na, g, p, nc = b_re.shape
    n = na * g
    ch = S5_CHUNK
    gps = S5_GROUPS_PER_STEP
    assert n % gps == 0
    bt = lambda a: jnp.swapaxes(a, 2, 3).reshape(n, nc, p)
    spec = lambda *shape: pl.BlockSpec((gps,) + shape, lambda i: (i,) + (0,) * len(shape))
    return pl.pallas_call(
        functools.partial(_s5_param_body, groups=gps, sample_len=sample_len),
        grid=(n // gps,),
        in_specs=[spec(1, p), spec(1, p), spec(1, 1),
                  spec(nc, p), spec(nc, p), spec(nc, p), spec(nc, p)],
        out_specs=[spec(nc * nc // 2, LANES), spec(nc * ch, 4 * p), spec(nc * ch, 2 * p),
                   spec(4, 2 * p)],
        out_shape=[
            jax.ShapeDtypeStruct((n, nc * nc // 2, LANES), F32),
            jax.ShapeDtypeStruct((n, nc * ch, 4 * p), BF16),
            jax.ShapeDtypeStruct((n, nc * ch, 2 * p), BF16),
            jax.ShapeDtypeStruct((n, 4, 2 * p), F32),
        ],
        compiler_params=_params(),
        name="s5_params",
    )(a_re.reshape(n, 1, p), a_im.reshape(n, 1, p), log_dt.reshape(n, 1, 1),
      bt(b_re), bt(b_im), c_re.reshape(n, nc, p), c_im.reshape(n, nc, p))


def _lane_is_low(shape):
    return lax.broadcasted_iota(jnp.int32, shape, len(shape) - 1) < S5_CHUNK


def _s5_in_body(x_ref, gpre_ref, wt_ref, u_ref, *, nc):
    tt = x_ref.shape[0]
    nq = nc // 2
    hn = _rms(x_ref[...], gpre_ref[0]).astype(BF16)
    ut = _dot_nt(wt_ref[0], hn)
    ut = ut.reshape(ut.shape[0] // nc, nc, tt)
    low = _lane_is_low((ut.shape[0], nq, LANES))
    for m in range(tt // LANES):
        a = ut[:, 0:nq, m * LANES:(m + 1) * LANES]
        b = ut[:, nq:nc, m * LANES:(m + 1) * LANES]
        u_ref[:, (2 * m) * nq:(2 * m + 1) * nq, :] = jnp.where(low, a, pltpu.roll(b, S5_CHUNK, 2))
        u_ref[:, (2 * m + 1) * nq:(2 * m + 2) * nq, :] = jnp.where(low, pltpu.roll(a, S5_CHUNK, 2), b)


def _s5_in(x, npre, wt, layer, j, nc, tt):
    nt, d = x.shape
    dm = wt.shape[1]
    g = dm // nc
    nidx = layer * 4 + 1
    rows_per_tile = tt // S5_CHUNK * (nc // 2)
    return pl.pallas_call(
        functools.partial(_s5_in_body, nc=nc),
        grid=(nt // tt,),
        in_specs=[
            pl.BlockSpec((tt, d), lambda i: (i, 0)),
            pl.BlockSpec((1, 1, d), lambda i: (nidx, 0, 0)),
            pl.BlockSpec((1, dm, d), lambda i: (j, 0, 0)),
        ],
        out_specs=pl.BlockSpec((g, rows_per_tile, LANES), lambda i: (0, i, 0)),
        out_shape=jax.ShapeDtypeStruct((g, nt // S5_CHUNK * (nc // 2), LANES), F32),
        compiler_params=_params(),
        name="s5_in",
    )(x, npre, wt)


def _s5_group(gi, u_ref, kmat_ref, wst_ref, cpow_ref, lam_ref, dl_ref, h0r_ref, h0i_ref,
              z_ref, hpr_ref, hpi_ref, hsr_ref, hsi_ref, t_ref, *, nc, n_prompt_rows,
              rows_per_stream, n_streams):
    ch = S5_CHUNK
    nq = nc // 2
    nrows = u_ref.shape[1] // nq
    p = h0r_ref.shape[2]
    npr = n_prompt_rows
    rps = rows_per_stream

    row_i = lax.broadcasted_iota(jnp.int32, (ch, LANES), 0)
    col_i = lax.broadcasted_iota(jnp.int32, (ch, LANES), 1)
    causal = (col_i % ch) >= row_i
    for slot in range(nc):
        kr = kmat_ref[gi, slot * nq:(slot + 1) * nq, :]
        for q in range(nq):
            blk = jnp.broadcast_to(kr[q:q + 1, :], (ch, LANES))
            blk = pltpu.roll(blk, 0, 1, stride=1, stride_axis=0)
            t_ref[gi, slot * ch:(slot + 1) * ch, q * LANES:(q + 1) * LANES] = (
                jnp.where(causal, blk, 0.0).astype(BF16))

    lhs = jnp.concatenate(
        [u_ref[gi, pl.ds(q, nrows, stride=nq), :] for q in range(nq)], axis=1)
    lhs_b = lhs.astype(BF16)

    y = _dot(lhs_b, t_ref[gi])
    hl = _dot(lhs_b, wst_ref[gi])
    hl_r, hl_i = hl[:, 0:2 * p], hl[:, 2 * p:4 * p]

    lam = lam_ref[gi]
    hr, hi = hl_r[0:npr], hl_i[0:npr]
    k_in_stream = lax.broadcasted_iota(jnp.int32, (npr, 1), 0) % rps
    ar, ai = lam[0:1, :], lam[1:2, :]
    dist = 1
    while dist < rps:
        sr = jnp.where(k_in_stream >= dist, pltpu.roll(hr, dist, 0), 0.0)
        si = jnp.where(k_in_stream >= dist, pltpu.roll(hi, dist, 0), 0.0)
        mr, mi = _cmul(ar, ai, sr, si)
        hr, hi = hr + mr, hi + mi
        ar, ai = _cmul(ar, ai, ar, ai)
        dist *= 2
    for b in range(n_streams):
        last = (b + 1) * rps - 1
        hpr_ref[gi, b:b + 1, :] = hr[last:last + 1, 0:p]
        hpi_ref[gi, b:b + 1, :] = hi[last:last + 1, 0:p]
    ser = jnp.where(k_in_stream >= 1, pltpu.roll(hr, 1, 0), 0.0)[:, 0:p]
    sei = jnp.where(k_in_stream >= 1, pltpu.roll(hi, 1, 0), 0.0)[:, 0:p]

    h0r, h0i = h0r_ref[gi], h0i_ref[gi]
    fr, fi = _cmul(lam[2:3, 0:p], lam[3:4, 0:p], h0r, h0i)
    hsr_ref[gi] = fr + hl_r[npr:, p:2 * p]
    hsi_ref[gi] = fi + hl_i[npr:, p:2 * p]

    s_all = jnp.concatenate([jnp.concatenate([ser, sei], axis=1),
                             jnp.concatenate([h0r, h0i], axis=1)], axis=0).astype(BF16)
    y = y + _dot_nt(s_all, cpow_ref[gi])

    z = jax.nn.gelu(y + dl_ref[gi] * lhs)
    for q in range(nq):
        z_ref[gi, pl.ds(q, nrows, stride=nq), :] = z[:, q * LANES:(q + 1) * LANES]


def _s5_scan_body(*refs, groups, **static):
    for gi in range(groups):
        _s5_group(gi, *refs, **static)


def _s5_scan(u, kmat, wst, cpow, lam, dl, h0r, h0i, j, nc, n_prompt_rows, rows_per_stream,
             n_streams):
    ch = S5_CHUNK
    g, blk, _ = u.shape
    gps = S5_GROUPS_PER_STEP
    assert g % gps == 0
    nrows = blk // (nc // 2)
    p = h0r.shape[2]
    nsr = nrows - n_prompt_rows
    body = functools.partial(_s5_scan_body, groups=gps, nc=nc, n_prompt_rows=n_prompt_rows,
                             rows_per_stream=rows_per_stream, n_streams=n_streams)
    j0 = j * (g // gps)
    par = lambda *shape: pl.BlockSpec((gps,) + shape, lambda i: (j0 + i,) + (0,) * len(shape))
    act = lambda *shape: pl.BlockSpec((gps,) + shape, lambda i: (i,) + (0,) * len(shape))
    return pl.pallas_call(
        body,
        grid=(g // gps,),
        in_specs=[
            act(blk, LANES),
            par(nc * nc // 2, LANES),
            par(nc * ch, 4 * p),
            par(nc * ch, 2 * p),
            par(4, 2 * p),
            par(1, nc * ch),
            par(nsr, p),
            par(nsr, p),
        ],
        out_specs=[act(blk, LANES), act(n_streams, p), act(n_streams, p), act(nsr, p), act(nsr, p)],
        out_shape=[
            jax.ShapeDtypeStruct(u.shape, F32),
            jax.ShapeDtypeStruct((g, n_streams, p), F32),
            jax.ShapeDtypeStruct((g, n_streams, p), F32),
            jax.ShapeDtypeStruct((g, nsr, p), F32),
            jax.ShapeDtypeStruct((g, nsr, p), F32),
        ],
        scratch_shapes=[pltpu.VMEM((gps, nc * ch, nc * ch), BF16)],
        compiler_params=_params(),
        name="s5_scan",
    )(u, kmat, wst, cpow, lam, dl, h0r, h0i)


def _s5_out_body(x_ref, z_ref, gpost_ref, wglu_ref, o_ref, *, nc):
    tt, d = x_ref.shape
    g = z_ref.shape[0]
    nq = nc // 2
    low = _lane_is_low((g, nq, LANES))
    blocks = []
    for m in range(tt // LANES):
        e = z_ref[:, (2 * m) * nq:(2 * m + 1) * nq, :]
        o = z_ref[:, (2 * m + 1) * nq:(2 * m + 2) * nq, :]
        a = jnp.where(low, e, pltpu.roll(o, S5_CHUNK, 2))
        b = jnp.where(low, pltpu.roll(e, S5_CHUNK, 2), o)
        blocks.append(jnp.concatenate([a, b], axis=1).reshape(g * nc, LANES))
    zt = jnp.concatenate(blocks, axis=1)
    gate = lax.dot_general(zt.astype(BF16), wglu_ref[0], _TN, preferred_element_type=F32)
    out = gate[:, :d] * jax.nn.sigmoid(gate[:, d:])
    o_ref[...] = x_ref[...] + _rms(out, gpost_ref[0])


def _s5_out(x, z, npost, wglu, layer, j, nc, tt):
    nt, d = x.shape
    dm = wglu.shape[1]
    g = dm // nc
    nidx = layer * 4 + 1
    rows_per_tile = tt // S5_CHUNK * (nc // 2)
    return pl.pallas_call(
        functools.partial(_s5_out_body, nc=nc),
        grid=(nt // tt,),
        in_specs=[
            pl.BlockSpec((tt, d), lambda i: (i, 0)),
            pl.BlockSpec((g, rows_per_tile, LANES), lambda i: (0, i, 0)),
            pl.BlockSpec((1, 1, d), lambda i: (nidx, 0, 0)),
            pl.BlockSpec((1, dm, 2 * d), lambda i: (j, 0, 0)),
        ],
        out_specs=pl.BlockSpec((tt, d), lambda i: (i, 0)),
        out_shape=jax.ShapeDtypeStruct(x.shape, x.dtype),
        compiler_params=_params(),
        name="s5_out",
    )(x, z, npost, wglu)


def kernel(x_prompt, x_sample, cache_mem_k, cache_mem_v, state_ssm_re, state_ssm_im, state_pool,
           mem_prompt, norm_pre, norm_post, norm_mem, ffn_w_gate, ffn_w_up, ffn_w_down,
           xa_w_q, xa_w_k, xa_w_v, xa_w_o,
           s5_w_in, s5_a_re, s5_a_im, s5_b_re, s5_b_im, s5_c_re, s5_c_im, s5_d, s5_log_dt, s5_w_glu,
           pool_w_in, pool_w_grp, pool_b_grp, pool_scale):
    b, seq, d = x_prompt.shape
    db, ds, _ = x_sample.shape
    depth, nsub, _ = norm_pre.shape
    nm, nh = cache_mem_k.shape[2], cache_mem_k.shape[3]
    na, g, p, nc = s5_b_re.shape
    nb_layers = pool_w_in.shape[0]
    f = ffn_w_gate.shape[-1]
    ch = S5_CHUNK
    assert ds <= ch and ds % SUBLANES == 0 and ds >= POOL_HIST and PAST_LEN >= POOL_HIST
    assert nc == 2 * SUBLANES and g * nc == d and 2 * p == LANES

    n_prompt = b * seq
    nt = n_prompt + db * ch
    tm = min(512, seq)
    assert seq % tm == 0 and tm % LANES == 0 and nt % tm == 0

    wg = ffn_w_gate.astype(BF16).reshape(depth * 2, d, f)
    wu = ffn_w_up.astype(BF16).reshape(depth * 2, d, f)
    wd = ffn_w_down.astype(BF16).reshape(depth * 2, f, d)
    wq, wk = xa_w_q.astype(BF16), xa_w_k.astype(BF16)
    wv, wo = xa_w_v.astype(BF16), xa_w_o.astype(BF16)
    s5_wt = jnp.swapaxes(s5_w_in, 1, 2).astype(BF16)
    s5_glu = s5_w_glu.astype(BF16)
    p_win, p_wgrp = pool_w_in.astype(BF16), pool_w_grp.astype(BF16)
    npre = norm_pre.reshape(depth * nsub, 1, d)
    npost = norm_post.reshape(depth * nsub, 1, d)

    assert db * ch == tm
    npt = n_prompt // tm
    xs = (x_prompt.reshape(n_prompt, d),
          jnp.pad(x_sample, ((0, 0), (0, ch - ds), (0, 0))).reshape(db * ch, d))

    mem_k_p, mem_v_p, kb_p, vb_p = _mem_kv(mem_prompt, norm_mem.reshape(depth, 1, d), wk, wv, nh)

    kmat, wst, cpow, lam = _s5_params(s5_a_re, s5_a_im, s5_log_dt, s5_b_re, s5_b_im,
                                      s5_c_re, s5_c_im, ds)
    slot_channel = [_slot_channel(slot, nc) for slot in range(nc)]
    dl = jnp.repeat(s5_d.reshape(na * g, 1, nc)[:, :, slot_channel], ch, axis=2)
    h0r = jnp.swapaxes(state_ssm_re, 1, 2).reshape(na * g, db, p)
    h0i = jnp.swapaxes(state_ssm_im, 1, 2).reshape(na * g, db, p)

    hist_p = jnp.zeros((nb_layers, b, HIST_ROWS, d), F32)
    hist_s = jnp.pad(state_pool, ((0, 0), (0, 0), (HIST_ROWS - POOL_HIST, 0), (0, 0)))
    pb = jnp.reshape(pool_b_grp, (nb_layers, 1, d))
    ps = jnp.reshape(pool_scale, (nb_layers, 1, d))

    ssm_p_re, ssm_p_im, ssm_s_re, ssm_s_im, buf_p, buf_s = [], [], [], [], [], []
    for i in range(depth):
        j = i // 2
        x = _ffn(xs if i == 0 else (x,), npre, npost, wg, wu, wd, i, 0, tm, npt)
        if i % 2 == 0:
            u = _s5_in(x, npre, s5_wt, i, j, nc, tm)
            z, hpr, hpi, hsr, hsi = _s5_scan(
                u, kmat, wst, cpow, lam, dl, h0r, h0i, j, nc, n_prompt // ch, seq // ch, b)
            x = _s5_out(x, z, npost, s5_glu, i, j, nc, tm)
            ssm_p_re.append(jnp.swapaxes(hpr, 0, 1))
            ssm_p_im.append(jnp.swapaxes(hpi, 0, 1))
            ssm_s_re.append(jnp.swapaxes(hsr, 0, 1))
            ssm_s_im.append(jnp.swapaxes(hsi, 0, 1))
        else:
            x, bp = _pool(x, npre, npost, p_win, p_wgrp, pb, ps, hist_p, i, j, 0, seq, tm, 0, tm)
            x, bs = _pool(x, npre, npost, p_win, p_wgrp, pb, ps, hist_s, i, j, n_prompt, ch, ds,
                          PAST_LEN, ch)
            buf_p.append(bp)
            buf_s.append(bs)
        x = _attn(x, npre, npost, wq, wo, kb_p, vb_p, i, nh, 0, seq, min(ATTN_TILE, seq))
        x = _attn(x, npre, npost, wq, wo, cache_mem_k, cache_mem_v, i, nh, n_prompt, ch, ch)
        x = _ffn((x,), npre, npost, wg, wu, wd, i, 1, tm, npt, split_out=(i == depth - 1))

    y_prompt = x[0].reshape(b, seq, d)
    y_sample = x[1].reshape(db, ch, d)[:, :ds, :]
    return (y_prompt, y_sample, mem_k_p, mem_v_p,
            jnp.stack(ssm_p_re), jnp.stack(ssm_p_im), jnp.stack(buf_p),
            jnp.stack(ssm_s_re), jnp.stack(ssm_s_im), jnp.stack(buf_s))
```

```python
import functools

import jax
import jax.numpy as jnp
from jax import lax
from jax.experimental import pallas as pl
from jax.experimental.pallas import tpu as pltpu

F32 = jnp.float32
BF16 = jnp.bfloat16

RMS_EPS = 1e-6
PAST_LEN = 1024
POOL_WINDOWS = (2, 4, 8, 16)
POOL_HIST = max(POOL_WINDOWS) - 1
HIST_ROWS = 16

LANES = 128
SUBLANES = 8
S5_CHUNK = LANES // 2
S5_GROUPS_PER_STEP = 2
FFN_ROW_CHUNK = 256
ATTN_TILE = 1024
VMEM_LIMIT = 56 * 1024 * 1024

_NT = (((1,), (1,)), ((), ()))
_TN = (((0,), (0,)), ((), ()))


def _params(n_axes=1, vmem=VMEM_LIMIT):
    return pltpu.CompilerParams(
        dimension_semantics=("arbitrary",) * n_axes, vmem_limit_bytes=vmem)


def _rms(x, g):
    ms = jnp.mean(x * x, axis=-1, keepdims=True)
    return x * lax.rsqrt(ms + RMS_EPS) * g


def _dot(a, b):
    return jnp.dot(a, b, preferred_element_type=F32)


def _w(ref, *idx):
    return ref[idx].astype(BF16)


def _dot_nt(a, b, precision=None):
    return lax.dot_general(a, b, _NT, preferred_element_type=F32, precision=precision)


def _ffn_body(*refs, n_in, n_out, n_prompt_tiles, row_chunk):
    x_refs, (gpre_ref, gpost_ref, wg_ref, wu_ref, wd_ref) = refs[:n_in], refs[n_in:n_in + 5]
    o_refs = refs[n_in + 5:]
    i = pl.program_id(0)
    tm = x_refs[0].shape[0]
    on_prompt = i < n_prompt_tiles
    for r in range(tm // row_chunk):
        rows = slice(r * row_chunk, (r + 1) * row_chunk)
        x = x_refs[0][rows, :]
        if n_in == 2:
            x = jnp.where(on_prompt, x, x_refs[1][rows, :])
        hn = _rms(x, gpre_ref[0]).astype(BF16)
        g = _dot(hn, _w(wg_ref, 0))
        u = _dot(hn, _w(wu_ref, 0))
        a = (g * jax.nn.sigmoid(g) * u).astype(BF16)
        o_refs[-1][rows, :] = x + 0.5 * _rms(_dot(a, _w(wd_ref, 0)), gpost_ref[0])
    if n_out == 2:
        @pl.when(on_prompt)
        def _():
            o_refs[0][...] = o_refs[1][...]


def _ffn(xs, npre, npost, wg, wu, wd, layer, which, tm, n_prompt_tiles, split_out=False):
    d = xs[0].shape[1]
    f = wg.shape[-1]
    sub = 0 if which == 0 else 3
    nsub = npre.shape[0] // (wg.shape[0] // 2)
    widx = layer * 2 + which
    nidx = layer * nsub + sub
    last_prompt = n_prompt_tiles - 1
    joint = pl.BlockSpec((tm, d), lambda i: (i, 0))
    prompt = pl.BlockSpec((tm, d), lambda i: (jnp.minimum(i, last_prompt), 0))
    sample = pl.BlockSpec((tm, d), lambda i: (0, 0))
    if len(xs) == 2:
        assert xs[0].shape[0] == n_prompt_tiles * tm and xs[1].shape[0] == tm
    nt = (n_prompt_tiles + 1) * tm
    if split_out:
        out_specs = [prompt, sample]
        out_shape = [jax.ShapeDtypeStruct((n_prompt_tiles * tm, d), F32),
                     jax.ShapeDtypeStruct((tm, d), F32)]
    else:
        out_specs = joint
        out_shape = jax.ShapeDtypeStruct((nt, d), F32)
    body = functools.partial(_ffn_body, n_in=len(xs), n_out=2 if split_out else 1,
                             n_prompt_tiles=n_prompt_tiles, row_chunk=FFN_ROW_CHUNK)
    return pl.pallas_call(
        body,
        grid=(nt // tm,),
        in_specs=([joint] if len(xs) == 1 else [prompt, sample]) + [
            pl.BlockSpec((1, 1, d), lambda i: (nidx, 0, 0)),
            pl.BlockSpec((1, 1, d), lambda i: (nidx, 0, 0)),
            pl.BlockSpec((1, d, f), lambda i: (widx, 0, 0), pipeline_mode=pl.Buffered(1)),
            pl.BlockSpec((1, d, f), lambda i: (widx, 0, 0), pipeline_mode=pl.Buffered(1)),
            pl.BlockSpec((1, f, d), lambda i: (widx, 0, 0), pipeline_mode=pl.Buffered(1)),
        ],
        out_specs=out_specs,
        out_shape=out_shape,
        compiler_params=_params(),
        name="ffn",
    )(*xs, npre, npost, wg, wu, wd)


def _kv_body(m_ref, g_ref, wk_ref, wv_ref, k_ref, v_ref, kb_ref, vb_ref):
    nh, hd = k_ref.shape[3], k_ref.shape[4]
    m = _rms(m_ref[0], g_ref[0]).astype(BF16)
    for w_ref, out_ref, bf_ref in ((wk_ref, k_ref, kb_ref), (wv_ref, v_ref, vb_ref)):
        y = _dot(m, _w(w_ref, 0))
        bf_ref[0, 0] = y.astype(BF16)
        for h in range(nh):
            out_ref[0, 0, :, h, :] = y[:, h * hd:(h + 1) * hd]


def _mem_kv(mem, nmem, wk, wv, nh):
    b, nm, d = mem.shape
    depth = wk.shape[0]
    hd = d // nh
    out = jax.ShapeDtypeStruct((depth, b, nm, nh, hd), F32)
    out_b = jax.ShapeDtypeStruct((depth, b, nm, d), BF16)
    return pl.pallas_call(
        _kv_body,
        grid=(depth, b),
        in_specs=[
            pl.BlockSpec((1, nm, d), lambda l, j: (j, 0, 0)),
            pl.BlockSpec((1, 1, d), lambda l, j: (l, 0, 0)),
            pl.BlockSpec((1, d, d), lambda l, j: (l, 0, 0)),
            pl.BlockSpec((1, d, d), lambda l, j: (l, 0, 0)),
        ],
        out_specs=[pl.BlockSpec((1, 1, nm, nh, hd), lambda l, j: (l, j, 0, 0, 0))] * 2
        + [pl.BlockSpec((1, 1, nm, d), lambda l, j: (l, j, 0, 0))] * 2,
        out_shape=[out, out, out_b, out_b],
        compiler_params=_params(2),
        name="mem_kv",
    )(mem, nmem, wk, wv)


def _attn_rows(x, gpre, gpost, wq, wo, kv, nh):
    hd = x.shape[-1] // nh
    hn = _rms(x, gpre).astype(BF16)
    q = (_dot(hn, wq) * (hd ** -0.5)).astype(BF16)
    heads = []
    for h, (kh, vh) in enumerate(kv):
        s = _dot_nt(q[:, h * hd:(h + 1) * hd], kh)
        e = jnp.exp(s - jnp.max(s, axis=-1, keepdims=True))
        p = e / jnp.sum(e, axis=-1, keepdims=True)
        heads.append(_dot(p.astype(BF16), vh))
    o = jnp.concatenate(heads, axis=-1).astype(BF16)
    return x + _rms(_dot(o, wo), gpost)


def _attn_body(x_ref, gpre_ref, gpost_ref, wq_ref, wo_ref, k_ref, v_ref, o_ref, *, nh):
    hd = x_ref.shape[1] // nh
    kv = [(k_ref[0, 0, :, h * hd:(h + 1) * hd], v_ref[0, 0, :, h * hd:(h + 1) * hd])
          for h in range(nh)]
    o_ref[...] = _attn_rows(x_ref[...], gpre_ref[0], gpost_ref[0], _w(wq_ref, 0), _w(wo_ref, 0),
                            kv, nh)


def _attn(x, npre, npost, wq, wo, k, v, layer, nh, rows_per_batch, ta):
    nt, d = x.shape
    depth, nb, nm, _ = k.shape
    nsub = npre.shape[0] // depth
    nidx = layer * nsub + 2
    tpb = rows_per_batch // ta
    kv_spec = pl.BlockSpec((1, 1, nm, d), lambda i: (layer, i // tpb, 0, 0))
    return pl.pallas_call(
        functools.partial(_attn_body, nh=nh),
        grid=(nb * tpb,),
        in_specs=[
            pl.BlockSpec((ta, d), lambda i: (i, 0)),
            pl.BlockSpec((1, 1, d), lambda i: (nidx, 0, 0)),
            pl.BlockSpec((1, 1, d), lambda i: (nidx, 0, 0)),
            pl.BlockSpec((1, d, d), lambda i: (layer, 0, 0)),
            pl.BlockSpec((1, d, d), lambda i: (layer, 0, 0)),
            kv_spec,
            kv_spec,
        ],
        out_specs=pl.BlockSpec((ta, d), lambda i: (i, 0)),
        out_shape=jax.ShapeDtypeStruct(x.shape, x.dtype),
        input_output_aliases={0: 0},
        compiler_params=_params(),
        name="xattn",
    )(x, npre, npost, wq, wo, k, v)


def _head_copies(k_hbm, v_hbm, kbuf, vbuf, sem, layer, stream, slot):
    copies = []
    for h in range(kbuf.shape[1]):
        copies.append(pltpu.make_async_copy(
            k_hbm.at[layer, stream, :, h, :], kbuf.at[slot, h], sem.at[slot, 0]))
        copies.append(pltpu.make_async_copy(
            v_hbm.at[layer, stream, :, h, :], vbuf.at[slot, h], sem.at[slot, 1]))
    return copies


def _attn_sample_body(x_ref, gpre_ref, gpost_ref, wq_ref, wo_ref, k_hbm, v_hbm, o_ref,
                      kbuf, vbuf, sem, *, layer):
    i = pl.program_id(0)
    nh = kbuf.shape[1]
    slot = i % 2

    @pl.when(i == 0)
    def _():
        for c in _head_copies(k_hbm, v_hbm, kbuf, vbuf, sem, layer, 0, 0):
            c.start()

    @pl.when(i + 1 < pl.num_programs(0))
    def _():
        for c in _head_copies(k_hbm, v_hbm, kbuf, vbuf, sem, layer, i + 1, 1 - slot):
            c.start()

    for c in _head_copies(k_hbm, v_hbm, kbuf, vbuf, sem, layer, i, slot):
        c.wait()
    kv = [(kbuf[slot, h].astype(BF16), vbuf[slot, h].astype(BF16)) for h in range(nh)]
    o_ref[...] = _attn_rows(x_ref[...], gpre_ref[0], gpost_ref[0], _w(wq_ref, 0), _w(wo_ref, 0),
                            kv, nh)


def _attn_sample(x, npre, npost, wq, wo, k, v, layer, row0, rows):
    nt, d = x.shape
    depth, nb, nm, nh, hd = k.shape
    nsub = npre.shape[0] // depth
    nidx = layer * nsub + 2
    t0 = row0 // rows
    return pl.pallas_call(
        functools.partial(_attn_sample_body, layer=layer),
        grid=(nb,),
        in_specs=[
            pl.BlockSpec((rows, d), lambda i: (t0 + i, 0)),
            pl.BlockSpec((1, 1, d), lambda i: (nidx, 0, 0)),
            pl.BlockSpec((1, 1, d), lambda i: (nidx, 0, 0)),
            pl.BlockSpec((1, d, d), lambda i: (layer, 0, 0)),
            pl.BlockSpec((1, d, d), lambda i: (layer, 0, 0)),
            pl.BlockSpec(memory_space=pl.ANY),
            pl.BlockSpec(memory_space=pl.ANY),
        ],
        out_specs=pl.BlockSpec((rows, d), lambda i: (t0 + i, 0)),
        out_shape=jax.ShapeDtypeStruct(x.shape, x.dtype),
        scratch_shapes=[pltpu.VMEM((2, nh, nm, hd), F32), pltpu.VMEM((2, nh, nm, hd), F32),
                        pltpu.SemaphoreType.DMA((2, 2))],
        input_output_aliases={0: 0},
        compiler_params=_params(),
        name="xattn_sample",
    )(x, npre, npost, wq, wo, k, v)


def _pool_body(x_ref, gpre_ref, gpost_ref, win_ref, wgrp_ref, bgrp_ref, scale_ref, hist_ref,
               o_ref, buf_ref, ext_ref, *, tiles_per_batch, pos0, valid_rows):
    i = pl.program_id(0)
    tt, d = x_ref.shape
    ng = len(POOL_WINDOWS)
    gw = d // ng
    first = (i % tiles_per_batch) == 0

    x = x_ref[...]
    hn = _rms(x, gpre_ref[0]).astype(BF16)
    u = _dot(hn, _w(win_ref, 0))

    @pl.when(first)
    def _():
        ext_ref[0:HIST_ROWS, :] = hist_ref[0, 0]

    @pl.when(jnp.logical_not(first))
    def _():
        ext_ref[0:HIST_ROWS, :] = ext_ref[tt:tt + HIST_ROWS, :]

    ext_ref[HIST_ROWS:HIST_ROWS + tt, :] = u

    tile_pos = pos0 + (i % tiles_per_batch) * tt
    pos = tile_pos + lax.broadcasted_iota(jnp.int32, (tt, 1), 0)
    level = ext_ref[...]
    parts = []
    for g, w in enumerate(POOL_WINDOWS):
        level = level + pltpu.roll(level, w // 2, 0)
        cnt = jnp.minimum(pos + 1, w).astype(F32)
        parts.append(level[HIST_ROWS:HIST_ROWS + tt, 0:gw] * (1.0 / cnt))
        if g + 1 < ng:
            level = level[:, gw:]
    pooled = (jnp.concatenate(parts, axis=-1) - u).astype(BF16)

    mixed = []
    for g in range(ng):
        sl = slice(g * gw, (g + 1) * gw)
        mixed.append(_dot(pooled[:, sl], _w(wgrp_ref, 0, g)) + bgrp_ref[0, :, sl])
    out = jnp.concatenate(mixed, axis=-1) * scale_ref[0]
    o_ref[...] = x + _rms(out, gpost_ref[0])

    @pl.when((i % tiles_per_batch) == tiles_per_batch - 1)
    def _():
        buf_ref[0] = u[valid_rows - POOL_HIST:valid_rows, :]


def _pool(x, npre, npost, win, wgrp, bgrp, scale, hist, layer, j, row0, rows_per_batch,
          valid_rows_last, pos0, tt):
    nt, d = x.shape
    nb = hist.shape[1]
    nidx = layer * 4 + 1
    ng = len(POOL_WINDOWS)
    gw = d // ng
    t0 = row0 // tt
    tpb = rows_per_batch // tt
    body = functools.partial(_pool_body, tiles_per_batch=tpb, pos0=pos0, valid_rows=valid_rows_last)
    return pl.pallas_call(
        body,
        grid=(nb * tpb,),
        in_specs=[
            pl.BlockSpec((tt, d), lambda i: (t0 + i, 0)),
            pl.BlockSpec((1, 1, d), lambda i: (nidx, 0, 0)),
            pl.BlockSpec((1, 1, d), lambda i: (nidx, 0, 0)),
            pl.BlockSpec((1, d, d), lambda i: (j, 0, 0)),
            pl.BlockSpec((1, ng, gw, gw), lambda i: (j, 0, 0, 0)),
            pl.BlockSpec((1, 1, d), lambda i: (j, 0, 0)),
            pl.BlockSpec((1, 1, d), lambda i: (j, 0, 0)),
            pl.BlockSpec((1, 1, HIST_ROWS, d), lambda i: (j, i // tpb, 0, 0)),
        ],
        out_specs=[
            pl.BlockSpec((tt, d), lambda i: (t0 + i, 0)),
            pl.BlockSpec((1, POOL_HIST, d), lambda i: (i // tpb, 0, 0)),
        ],
        out_shape=[
            jax.ShapeDtypeStruct(x.shape, x.dtype),
            jax.ShapeDtypeStruct((nb, POOL_HIST, d), F32),
        ],
        scratch_shapes=[pltpu.VMEM((HIST_ROWS + tt, d), F32)],
        input_output_aliases={0: 0},
        compiler_params=_params(),
        name="pool_mixer",
    )(x, npre, npost, win, wgrp, bgrp, scale, hist)


def _cmul(ar, ai, br, bi):
    return ar * br - ai * bi, ar * bi + ai * br


def _cpow(zr, zi, n):
    rr, ri = None, None
    while n:
        if n & 1:
            rr, ri = (zr, zi) if rr is None else _cmul(rr, ri, zr, zi)
        n >>= 1
        if n:
            zr, zi = _cmul(zr, zi, zr, zi)
    return rr, ri


def _slot_channel(slot, nc):
    return slot // 2 + (nc // 2) * (slot % 2)


def _s5_param_group(gi, are_ref, aim_ref, ldt_ref, btr_ref, bti_ref, cr_ref, ci_ref,
                    kmat_ref, wst_ref, cpow_ref, lam_ref, *, sample_len):
    ch = S5_CHUNK
    nc, p = btr_ref.shape[1], btr_ref.shape[2]
    nq = nc // 2
    dt = jnp.exp(ldt_ref[gi])
    lre, lim = are_ref[gi], aim_ref[gi]
    zr, zi = lre * dt, lim * dt
    mag = jnp.exp(zr)
    lbr, lbi = mag * jnp.cos(zi), mag * jnp.sin(zi)
    den = lre * lre + lim * lim
    qr = ((lbr - 1.0) * lre + lbi * lim) / den
    qi = (lbi * lre - (lbr - 1.0) * lim) / den
    bbr, bbi = _cmul(qr, qi, btr_ref[gi], bti_ref[gi])

    n8 = lax.broadcasted_iota(jnp.int32, (SUBLANES, 1), 0)

    def base(expo):
        m = jnp.exp(expo * zr)
        a = expo * zi
        return m * jnp.cos(a), m * jnp.sin(a)

    asc_r, asc_i = base(n8.astype(F32))
    rev_r, rev_i = base((SUBLANES - 1 - n8).astype(F32))
    rows = SUBLANES
    while rows < ch:
        sr, si = _cpow(lbr, lbi, rows)
        ur, ui = _cmul(asc_r, asc_i, sr, si)
        asc_r, asc_i = jnp.concatenate([asc_r, ur], axis=0), jnp.concatenate([asc_i, ui], axis=0)
        ur, ui = _cmul(rev_r, rev_i, sr, si)
        rev_r, rev_i = jnp.concatenate([ur, rev_r], axis=0), jnp.concatenate([ui, rev_i], axis=0)
        rows *= 2
    p1r, p1i = _cmul(asc_r, asc_i, lbr, lbi)
    p1_ri = jnp.concatenate([p1r, p1i], axis=1)
    p1_ir = jnp.concatenate([p1i, p1r], axis=1)

    dead = jnp.zeros((ch - sample_len, p), F32)
    rev2_r = jnp.concatenate(
        [rev_r, jnp.concatenate([rev_r[ch - sample_len:], dead], axis=0)], axis=1)
    rev2_i = jnp.concatenate(
        [rev_i, jnp.concatenate([rev_i[ch - sample_len:], dead], axis=0)], axis=1)
    bb2_r = jnp.concatenate([bbr, bbr], axis=1)
    bb2_i = jnp.concatenate([bbi, bbi], axis=1)
    cr, ci = cr_ref[gi], ci_ref[gi]
    c_pm = jnp.concatenate([cr, -cr], axis=1)
    c_mm = jnp.concatenate([-ci, -ci], axis=1)
    w_r = [[], []]
    w_i = [[], []]
    for slot in range(nc):
        c = _slot_channel(slot, nc)
        wr, wi = _cmul(rev2_r, rev2_i, bb2_r[c:c + 1, :], bb2_i[c:c + 1, :])
        wst_ref[gi, slot * ch:(slot + 1) * ch, :] = jnp.concatenate([wr, wi], axis=1).astype(BF16)
        cpow_ref[gi, slot * ch:(slot + 1) * ch, :] = (
            p1_ri * c_pm[c:c + 1, :] + p1_ir * c_mm[c:c + 1, :]).astype(BF16)
        for h in range(2):
            r_, i_ = _cmul(cr[h * nq:(h + 1) * nq, :], ci[h * nq:(h + 1) * nq, :],
                           bbr[c:c + 1, :], bbi[c:c + 1, :])
            w_r[h].append(r_)
            w_i[h].append(i_)

    hi = lax.Precision.HIGHEST
    halves = [_dot_nt(jnp.concatenate(w_r[h], axis=0), asc_r, hi)
              - _dot_nt(jnp.concatenate(w_i[h], axis=0), asc_i, hi) for h in range(2)]
    kmat_ref[gi] = jnp.concatenate(halves, axis=1)

    llr, lli = _cpow(lbr, lbi, ch)
    lsr, lsi = _cpow(lbr, lbi, sample_len)
    lam = jnp.concatenate([llr, lli, lsr, lsi], axis=0)
    lam_ref[gi] = jnp.concatenate([lam, lam], axis=1)


def _s5_param_body(*refs, groups, sample_len):
    for gi in range(groups):
        _s5_param_group(gi, *refs, sample_len=sample_len)


def _s5_params(a_re, a_im, log_dt, b_re, b_im, c_re, c_im, sample_len):
    na, g, p, nc = b_re.shape
    n = na * g
    ch = S5_CHUNK
    gps = S5_GROUPS_PER_STEP
    assert n % gps == 0
    bt = lambda a: jnp.swapaxes(a, 2, 3).reshape(n, nc, p)
    spec = lambda *shape: pl.BlockSpec((gps,) + shape, lambda i: (i,) + (0,) * len(shape))
    return pl.pallas_call(
        functools.partial(_s5_param_body, groups=gps, sample_len=sample_len),
        grid=(n // gps,),
        in_specs=[spec(1, p), spec(1, p), spec(1, 1),
                  spec(nc, p), spec(nc, p), spec(nc, p), spec(nc, p)],
        out_specs=[spec(nc * nc // 2, LANES), spec(nc * ch, 4 * p), spec(nc * ch, 2 * p),
                   spec(4, 2 * p)],
        out_shape=[
            jax.ShapeDtypeStruct((n, nc * nc // 2, LANES), F32),
            jax.ShapeDtypeStruct((n, nc * ch, 4 * p), BF16),
            jax.ShapeDtypeStruct((n, nc * ch, 2 * p), BF16),
            jax.ShapeDtypeStruct((n, 4, 2 * p), F32),
        ],
        compiler_params=_params(),
        name="s5_params",
    )(a_re.reshape(n, 1, p), a_im.reshape(n, 1, p), log_dt.reshape(n, 1, 1),
      bt(b_re), bt(b_im), c_re.reshape(n, nc, p), c_im.reshape(n, nc, p))


def _lane_is_low(shape):
    return lax.broadcasted_iota(jnp.int32, shape, len(shape) - 1) < S5_CHUNK


def _s5_in_body(x_ref, gpre_ref, wt_ref, u_ref, *, nc):
    tt = x_ref.shape[0]
    nq = nc // 2
    hn = _rms(x_ref[...], gpre_ref[0]).astype(BF16)
    ut = _dot_nt(wt_ref[0], hn)
    ut = ut.reshape(ut.shape[0] // nc, nc, tt)
    low = _lane_is_low((ut.shape[0], nq, LANES))
    for m in range(tt // LANES):
        a = ut[:, 0:nq, m * LANES:(m + 1) * LANES]
        b = ut[:, nq:nc, m * LANES:(m + 1) * LANES]
        u_ref[:, (2 * m) * nq:(2 * m + 1) * nq, :] = jnp.where(low, a, pltpu.roll(b, S5_CHUNK, 2))
        u_ref[:, (2 * m + 1) * nq:(2 * m + 2) * nq, :] = jnp.where(low, pltpu.roll(a, S5_CHUNK, 2), b)


def _s5_in(x, npre, wt, layer, j, nc, tt):
    nt, d = x.shape
    dm = wt.shape[1]
    g = dm // nc
    nidx = layer * 4 + 1
    rows_per_tile = tt // S5_CHUNK * (nc // 2)
    return pl.pallas_call(
        functools.partial(_s5_in_body, nc=nc),
        grid=(nt // tt,),
        in_specs=[
            pl.BlockSpec((tt, d), lambda i: (i, 0)),
            pl.BlockSpec((1, 1, d), lambda i: (nidx, 0, 0)),
            pl.BlockSpec((1, dm, d), lambda i: (j, 0, 0)),
        ],
        out_specs=pl.BlockSpec((g, rows_per_tile, LANES), lambda i: (0, i, 0)),
        out_shape=jax.ShapeDtypeStruct((g, nt // S5_CHUNK * (nc // 2), LANES), F32),
        compiler_params=_params(),
        name="s5_in",
    )(x, npre, wt)


def _s5_group(gi, u_ref, kmat_ref, wst_ref, cpow_ref, lam_ref, dl_ref, h0r_ref, h0i_ref,
              z_ref, hpr_ref, hpi_ref, hsr_ref, hsi_ref, t_ref, *, nc, n_prompt_rows,
              rows_per_stream, n_streams):
    ch = S5_CHUNK
    nq = nc // 2
    nrows = u_ref.shape[1] // nq
    p = h0r_ref.shape[2]
    npr = n_prompt_rows
    rps = rows_per_stream

    row_i = lax.broadcasted_iota(jnp.int32, (ch, LANES), 0)
    col_i = lax.broadcasted_iota(jnp.int32, (ch, LANES), 1)
    causal = (col_i % ch) >= row_i
    for slot in range(nc):
        kr = kmat_ref[gi, slot * nq:(slot + 1) * nq, :]
        for q in range(nq):
            blk = jnp.broadcast_to(kr[q:q + 1, :], (ch, LANES))
            blk = pltpu.roll(blk, 0, 1, stride=1, stride_axis=0)
            t_ref[gi, slot * ch:(slot + 1) * ch, q * LANES:(q + 1) * LANES] = (
                jnp.where(causal, blk, 0.0).astype(BF16))

    lhs = jnp.concatenate(
        [u_ref[gi, pl.ds(q, nrows, stride=nq), :] for q in range(nq)], axis=1)
    lhs_b = lhs.astype(BF16)

    y = _dot(lhs_b, t_ref[gi])
    hl = _dot(lhs_b, wst_ref[gi])
    hl_r, hl_i = hl[:, 0:2 * p], hl[:, 2 * p:4 * p]

    lam = lam_ref[gi]
    hr, hi = hl_r[0:npr], hl_i[0:npr]
    k_in_stream = lax.broadcasted_iota(jnp.int32, (npr, 1), 0) % rps
    ar, ai = lam[0:1, :], lam[1:2, :]
    dist = 1
    while dist < rps:
        sr = jnp.where(k_in_stream >= dist, pltpu.roll(hr, dist, 0), 0.0)
        si = jnp.where(k_in_stream >= dist, pltpu.roll(hi, dist, 0), 0.0)
        mr, mi = _cmul(ar, ai, sr, si)
        hr, hi = hr + mr, hi + mi
        ar, ai = _cmul(ar, ai, ar, ai)
        dist *= 2
    for b in range(n_streams):
        last = (b + 1) * rps - 1
        hpr_ref[gi, b:b + 1, :] = hr[last:last + 1, 0:p]
        hpi_ref[gi, b:b + 1, :] = hi[last:last + 1, 0:p]
    ser = jnp.where(k_in_stream >= 1, pltpu.roll(hr, 1, 0), 0.0)[:, 0:p]
    sei = jnp.where(k_in_stream >= 1, pltpu.roll(hi, 1, 0), 0.0)[:, 0:p]

    h0r, h0i = h0r_ref[gi], h0i_ref[gi]
    fr, fi = _cmul(lam[2:3, 0:p], lam[3:4, 0:p], h0r, h0i)
    hsr_ref[gi] = fr + hl_r[npr:, p:2 * p]
    hsi_ref[gi] = fi + hl_i[npr:, p:2 * p]

    s_all = jnp.concatenate([jnp.concatenate([ser, sei], axis=1),
                             jnp.concatenate([h0r, h0i], axis=1)], axis=0).astype(BF16)
    y = y + _dot_nt(s_all, cpow_ref[gi])

    z = jax.nn.gelu(y + dl_ref[gi] * lhs)
    for q in range(nq):
        z_ref[gi, pl.ds(q, nrows, stride=nq), :] = z[:, q * LANES:(q + 1) * LANES]


def _s5_scan_body(*refs, groups, **static):
    for gi in range(groups):
        _s5_group(gi, *refs, **static)


def _s5_scan(u, kmat, wst, cpow, lam, dl, h0r, h0i, j, nc, n_prompt_rows, rows_per_stream,
             n_streams):
    ch = S5_CHUNK
    g, blk, _ = u.shape
    gps = S5_GROUPS_PER_STEP
    assert g % gps == 0
    nrows = blk // (nc // 2)
    p = h0r.shape[2]
    nsr = nrows - n_prompt_rows
    body = functools.partial(_s5_scan_body, groups=gps, nc=nc, n_prompt_rows=n_prompt_rows,
                             rows_per_stream=rows_per_stream, n_streams=n_streams)
    j0 = j * (g // gps)
    par = lambda *shape: pl.BlockSpec((gps,) + shape, lambda i: (j0 + i,) + (0,) * len(shape))
    act = lambda *shape: pl.BlockSpec((gps,) + shape, lambda i: (i,) + (0,) * len(shape))
    return pl.pallas_call(
        body,
        grid=(g // gps,),
        in_specs=[
            act(blk, LANES),
            par(nc * nc // 2, LANES),
            par(nc * ch, 4 * p),
            par(nc * ch, 2 * p),
            par(4, 2 * p),
            par(1, nc * ch),
            par(nsr, p),
            par(nsr, p),
        ],
        out_specs=[act(blk, LANES), act(n_streams, p), act(n_streams, p), act(nsr, p), act(nsr, p)],
        out_shape=[
            jax.ShapeDtypeStruct(u.shape, F32),
            jax.ShapeDtypeStruct((g, n_streams, p), F32),
            jax.ShapeDtypeStruct((g, n_streams, p), F32),
            jax.ShapeDtypeStruct((g, nsr, p), F32),
            jax.ShapeDtypeStruct((g, nsr, p), F32),
        ],
        scratch_shapes=[pltpu.VMEM((gps, nc * ch, nc * ch), BF16)],
        compiler_params=_params(),
        name="s5_scan",
    )(u, kmat, wst, cpow, lam, dl, h0r, h0i)


def _s5_out_body(x_ref, z_ref, gpost_ref, wglu_ref, o_ref, *, nc):
    tt, d = x_ref.shape
    g = z_ref.shape[0]
    nq = nc // 2
    low = _lane_is_low((g, nq, LANES))
    blocks = []
    for m in range(tt // LANES):
        e = z_ref[:, (2 * m) * nq:(2 * m + 1) * nq, :]
        o = z_ref[:, (2 * m + 1) * nq:(2 * m + 2) * nq, :]
        a = jnp.where(low, e, pltpu.roll(o, S5_CHUNK, 2))
        b = jnp.where(low, pltpu.roll(e, S5_CHUNK, 2), o)
        blocks.append(jnp.concatenate([a, b], axis=1).reshape(g * nc, LANES))
    zt = jnp.concatenate(blocks, axis=1)
    gate = lax.dot_general(zt.astype(BF16), _w(wglu_ref, 0), _TN, preferred_element_type=F32)
    out = gate[:, :d] * jax.nn.sigmoid(gate[:, d:])
    o_ref[...] = x_ref[...] + _rms(out, gpost_ref[0])


def _s5_out(x, z, npost, wglu, layer, j, nc, tt):
    nt, d = x.shape
    dm = wglu.shape[1]
    g = dm // nc
    nidx = layer * 4 + 1
    rows_per_tile = tt // S5_CHUNK * (nc // 2)
    return pl.pallas_call(
        functools.partial(_s5_out_body, nc=nc),
        grid=(nt // tt,),
        in_specs=[
            pl.BlockSpec((tt, d), lambda i: (i, 0)),
            pl.BlockSpec((g, rows_per_tile, LANES), lambda i: (0, i, 0)),
            pl.BlockSpec((1, 1, d), lambda i: (nidx, 0, 0)),
            pl.BlockSpec((1, dm, 2 * d), lambda i: (j, 0, 0)),
        ],
        out_specs=pl.BlockSpec((tt, d), lambda i: (i, 0)),
        out_shape=jax.ShapeDtypeStruct(x.shape, x.dtype),
        compiler_params=_params(),
        name="s5_out",
    )(x, z, npost, wglu)


def kernel(x_prompt, x_sample, cache_mem_k, cache_mem_v, state_ssm_re, state_ssm_im, state_pool,
           mem_prompt, norm_pre, norm_post, norm_mem, ffn_w_gate, ffn_w_up, ffn_w_down,
           xa_w_q, xa_w_k, xa_w_v, xa_w_o,
           s5_w_in, s5_a_re, s5_a_im, s5_b_re, s5_b_im, s5_c_re, s5_c_im, s5_d, s5_log_dt, s5_w_glu,
           pool_w_in, pool_w_grp, pool_b_grp, pool_scale):
    b, seq, d = x_prompt.shape
    db, ds, _ = x_sample.shape
    depth, nsub, _ = norm_pre.shape
    nm, nh = cache_mem_k.shape[2], cache_mem_k.shape[3]
    na, g, p, nc = s5_b_re.shape
    nb_layers = pool_w_in.shape[0]
    f = ffn_w_gate.shape[-1]
    ch = S5_CHUNK
    assert ds <= ch and ds % SUBLANES == 0 and ds >= POOL_HIST and PAST_LEN >= POOL_HIST
    assert nc == 2 * SUBLANES and g * nc == d and 2 * p == LANES

    n_prompt = b * seq
    nt = n_prompt + db * ch
    tm = min(512, seq)
    assert seq % tm == 0 and tm % LANES == 0 and nt % tm == 0

    wg = ffn_w_gate.reshape(depth * 2, d, f)
    wu = ffn_w_up.reshape(depth * 2, d, f)
    wd = ffn_w_down.reshape(depth * 2, f, d)
    wq, wk, wv, wo = xa_w_q, xa_w_k, xa_w_v, xa_w_o
    s5_wt = jnp.swapaxes(s5_w_in, 1, 2).astype(BF16)
    s5_glu = s5_w_glu
    p_win, p_wgrp = pool_w_in, pool_w_grp
    npre = norm_pre.reshape(depth * nsub, 1, d)
    npost = norm_post.reshape(depth * nsub, 1, d)

    assert db * ch == tm
    npt = n_prompt // tm
    xs = (x_prompt.reshape(n_prompt, d),
          jnp.pad(x_sample, ((0, 0), (0, ch - ds), (0, 0))).reshape(db * ch, d))

    mem_k_p, mem_v_p, kb_p, vb_p = _mem_kv(mem_prompt, norm_mem.reshape(depth, 1, d), wk, wv, nh)

    kmat, wst, cpow, lam = _s5_params(s5_a_re, s5_a_im, s5_log_dt, s5_b_re, s5_b_im,
                                      s5_c_re, s5_c_im, ds)
    slot_channel = [_slot_channel(slot, nc) for slot in range(nc)]
    dl = jnp.repeat(s5_d.reshape(na * g, 1, nc)[:, :, slot_channel], ch, axis=2)
    h0r = jnp.swapaxes(state_ssm_re, 1, 2).reshape(na * g, db, p)
    h0i = jnp.swapaxes(state_ssm_im, 1, 2).reshape(na * g, db, p)

    hist_p = jnp.zeros((nb_layers, b, HIST_ROWS, d), F32)
    hist_s = jnp.pad(state_pool, ((0, 0), (0, 0), (HIST_ROWS - POOL_HIST, 0), (0, 0)))
    pb = jnp.reshape(pool_b_grp, (nb_layers, 1, d))
    ps = jnp.reshape(pool_scale, (nb_layers, 1, d))

    ssm_p_re, ssm_p_im, ssm_s_re, ssm_s_im, buf_p, buf_s = [], [], [], [], [], []
    for i in range(depth):
        j = i // 2
        x = _ffn(xs if i == 0 else (x,), npre, npost, wg, wu, wd, i, 0, tm, npt)
        if i % 2 == 0:
            u = _s5_in(x, npre, s5_wt, i, j, nc, tm)
            z, hpr, hpi, hsr, hsi = _s5_scan(
                u, kmat, wst, cpow, lam, dl, h0r, h0i, j, nc, n_prompt // ch, seq // ch, b)
            x = _s5_out(x, z, npost, s5_glu, i, j, nc, tm)
            ssm_p_re.append(jnp.swapaxes(hpr, 0, 1))
            ssm_p_im.append(jnp.swapaxes(hpi, 0, 1))
            ssm_s_re.append(jnp.swapaxes(hsr, 0, 1))
            ssm_s_im.append(jnp.swapaxes(hsi, 0, 1))
        else:
            x, bp = _pool(x, npre, npost, p_win, p_wgrp, pb, ps, hist_p, i, j, 0, seq, tm, 0, tm)
            x, bs = _pool(x, npre, npost, p_win, p_wgrp, pb, ps, hist_s, i, j, n_prompt, ch, ds,
                          PAST_LEN, ch)
            buf_p.append(bp)
            buf_s.append(bs)
        x = _attn(x, npre, npost, wq, wo, kb_p, vb_p, i, nh, seq, min(ATTN_TILE, seq))
        x = _attn_sample(x, npre, npost, wq, wo, cache_mem_k, cache_mem_v, i, n_prompt, ch)
        x = _ffn((x,), npre, npost, wg, wu, wd, i, 1, tm, npt, split_out=(i == depth - 1))

    y_prompt = x[0].reshape(b, seq, d)
    y_sample = x[1].reshape(db, ch, d)[:, :ds, :]
    return (y_prompt, y_sample, mem_k_p, mem_v_p,
            jnp.stack(ssm_p_re), jnp.stack(ssm_p_im), jnp.stack(buf_p),
            jnp.stack(ssm_s_re), jnp.stack(ssm_s_im), jnp.stack(buf_s))
```

```python
import functools

import jax
import jax.numpy as jnp
from jax import lax
from jax.experimental import pallas as pl
from jax.experimental.pallas import tpu as pltpu

F32 = jnp.float32
BF16 = jnp.bfloat16

RMS_EPS = 1e-6
PAST_LEN = 1024
POOL_WINDOWS = (2, 4, 8, 16)
POOL_HIST = max(POOL_WINDOWS) - 1
HIST_ROWS = 16

LANES = 128
SUBLANES = 8
S5_CHUNK = LANES // 2
S5_GROUPS_PER_STEP = 2
FFN_ROW_CHUNK = 256
ATTN_TILE = 1024
POOL_TILE = 1024
VMEM_LIMIT = 56 * 1024 * 1024

_NT = (((1,), (1,)), ((), ()))
_TN = (((0,), (0,)), ((), ()))


def _params(n_axes=1, vmem=VMEM_LIMIT):
    return pltpu.CompilerParams(
        dimension_semantics=("arbitrary",) * n_axes, vmem_limit_bytes=vmem)


def _rms(x, g):
    ms = jnp.mean(x * x, axis=-1, keepdims=True)
    return x * lax.rsqrt(ms + RMS_EPS) * g


def _dot(a, b):
    return jnp.dot(a, b, preferred_element_type=F32)


def _w(ref, *idx):
    return ref[idx].astype(BF16)


def _dot_nt(a, b, precision=None):
    return lax.dot_general(a, b, _NT, preferred_element_type=F32, precision=precision)


def _ffn_body(*refs, n_in, n_out, n_prompt_tiles, row_chunk):
    x_refs, (gpre_ref, gpost_ref, wg_ref, wu_ref, wd_ref) = refs[:n_in], refs[n_in:n_in + 5]
    o_refs = refs[n_in + 5:]
    i = pl.program_id(0)
    tm = x_refs[0].shape[0]
    on_prompt = i < n_prompt_tiles
    for r in range(tm // row_chunk):
        rows = slice(r * row_chunk, (r + 1) * row_chunk)
        x = x_refs[0][rows, :]
        if n_in == 2:
            x = jnp.where(on_prompt, x, x_refs[1][rows, :])
        hn = _rms(x, gpre_ref[0]).astype(BF16)
        g = _dot(hn, _w(wg_ref, 0))
        u = _dot(hn, _w(wu_ref, 0))
        a = (g * jax.nn.sigmoid(g) * u).astype(BF16)
        o_refs[-1][rows, :] = x + 0.5 * _rms(_dot(a, _w(wd_ref, 0)), gpost_ref[0])
    if n_out == 2:
        @pl.when(on_prompt)
        def _():
            o_refs[0][...] = o_refs[1][...]


def _ffn(xs, npre, npost, wg, wu, wd, layer, which, tm, n_prompt_tiles, split_out=False):
    d = xs[0].shape[1]
    f = wg.shape[-1]
    sub = 0 if which == 0 else 3
    nsub = npre.shape[0] // (wg.shape[0] // 2)
    widx = layer * 2 + which
    nidx = layer * nsub + sub
    last_prompt = n_prompt_tiles - 1
    joint = pl.BlockSpec((tm, d), lambda i: (i, 0))
    prompt = pl.BlockSpec((tm, d), lambda i: (jnp.minimum(i, last_prompt), 0))
    sample = pl.BlockSpec((tm, d), lambda i: (0, 0))
    once = dict(pipeline_mode=pl.Buffered(1))
    if len(xs) == 2:
        assert xs[0].shape[0] == n_prompt_tiles * tm and xs[1].shape[0] == tm
    nt = (n_prompt_tiles + 1) * tm
    if split_out:
        out_specs = [prompt, sample]
        out_shape = [jax.ShapeDtypeStruct((n_prompt_tiles * tm, d), F32),
                     jax.ShapeDtypeStruct((tm, d), F32)]
    else:
        out_specs = joint
        out_shape = jax.ShapeDtypeStruct((nt, d), F32)
    body = functools.partial(_ffn_body, n_in=len(xs), n_out=2 if split_out else 1,
                             n_prompt_tiles=n_prompt_tiles, row_chunk=FFN_ROW_CHUNK)
    return pl.pallas_call(
        body,
        grid=(nt // tm,),
        in_specs=([joint] if len(xs) == 1 else [prompt, sample]) + [
            pl.BlockSpec((1, 1, d), lambda i: (nidx, 0, 0)),
            pl.BlockSpec((1, 1, d), lambda i: (nidx, 0, 0)),
            pl.BlockSpec((1, d, f), lambda i: (widx, 0, 0), **once),
            pl.BlockSpec((1, d, f), lambda i: (widx, 0, 0), **once),
            pl.BlockSpec((1, f, d), lambda i: (widx, 0, 0), **once),
        ],
        out_specs=out_specs,
        out_shape=out_shape,
        compiler_params=_params(),
        name="ffn",
    )(*xs, npre, npost, wg, wu, wd)


def _kv_body(m_ref, g_ref, wk_ref, wv_ref, k_ref, v_ref, kb_ref, vb_ref):
    nh, hd = k_ref.shape[3], k_ref.shape[4]
    m = _rms(m_ref[0], g_ref[0]).astype(BF16)
    for w_ref, out_ref, bf_ref in ((wk_ref, k_ref, kb_ref), (wv_ref, v_ref, vb_ref)):
        y = _dot(m, _w(w_ref, 0))
        bf_ref[0, 0] = y.astype(BF16)
        for h in range(nh):
            out_ref[0, 0, :, h, :] = y[:, h * hd:(h + 1) * hd]


def _mem_kv(mem, nmem, wk, wv, nh):
    b, nm, d = mem.shape
    depth = wk.shape[0]
    hd = d // nh
    out = jax.ShapeDtypeStruct((depth, b, nm, nh, hd), F32)
    out_b = jax.ShapeDtypeStruct((depth, b, nm, d), BF16)
    return pl.pallas_call(
        _kv_body,
        grid=(depth, b),
        in_specs=[
            pl.BlockSpec((1, nm, d), lambda l, j: (j, 0, 0)),
            pl.BlockSpec((1, 1, d), lambda l, j: (l, 0, 0)),
            pl.BlockSpec((1, d, d), lambda l, j: (l, 0, 0)),
            pl.BlockSpec((1, d, d), lambda l, j: (l, 0, 0)),
        ],
        out_specs=[pl.BlockSpec((1, 1, nm, nh, hd), lambda l, j: (l, j, 0, 0, 0))] * 2
        + [pl.BlockSpec((1, 1, nm, d), lambda l, j: (l, j, 0, 0))] * 2,
        out_shape=[out, out, out_b, out_b],
        compiler_params=_params(2),
        name="mem_kv",
    )(mem, nmem, wk, wv)


def _attn_rows(x, gpre, gpost, wq, wo, kv, nh):
    hd = x.shape[-1] // nh
    hn = _rms(x, gpre).astype(BF16)
    q = (_dot(hn, wq) * (hd ** -0.5)).astype(BF16)
    heads = []
    for h, (kh, vh) in enumerate(kv):
        s = _dot_nt(q[:, h * hd:(h + 1) * hd], kh)
        e = jnp.exp(s - jnp.max(s, axis=-1, keepdims=True))
        p = e / jnp.sum(e, axis=-1, keepdims=True)
        heads.append(_dot(p.astype(BF16), vh))
    o = jnp.concatenate(heads, axis=-1).astype(BF16)
    return x + _rms(_dot(o, wo), gpost)


def _attn_body(x_ref, gpre_ref, gpost_ref, wq_ref, wo_ref, k_ref, v_ref, o_ref, *, nh):
    hd = x_ref.shape[1] // nh
    kv = [(k_ref[0, 0, :, h * hd:(h + 1) * hd], v_ref[0, 0, :, h * hd:(h + 1) * hd])
          for h in range(nh)]
    o_ref[...] = _attn_rows(x_ref[...], gpre_ref[0], gpost_ref[0], _w(wq_ref, 0), _w(wo_ref, 0),
                            kv, nh)


def _attn(x, npre, npost, wq, wo, k, v, layer, nh, rows_per_batch, ta):
    nt, d = x.shape
    depth, nb, nm, _ = k.shape
    nsub = npre.shape[0] // depth
    nidx = layer * nsub + 2
    tpb = rows_per_batch // ta
    kv_spec = pl.BlockSpec((1, 1, nm, d), lambda i: (layer, i // tpb, 0, 0))
    return pl.pallas_call(
        functools.partial(_attn_body, nh=nh),
        grid=(nb * tpb,),
        in_specs=[
            pl.BlockSpec((ta, d), lambda i: (i, 0)),
            pl.BlockSpec((1, 1, d), lambda i: (nidx, 0, 0)),
            pl.BlockSpec((1, 1, d), lambda i: (nidx, 0, 0)),
            pl.BlockSpec((1, d, d), lambda i: (layer, 0, 0), pipeline_mode=pl.Buffered(1)),
            pl.BlockSpec((1, d, d), lambda i: (layer, 0, 0), pipeline_mode=pl.Buffered(1)),
            kv_spec,
            kv_spec,
        ],
        out_specs=pl.BlockSpec((ta, d), lambda i: (i, 0)),
        out_shape=jax.ShapeDtypeStruct(x.shape, x.dtype),
        input_output_aliases={0: 0},
        compiler_params=_params(),
        name="xattn",
    )(x, npre, npost, wq, wo, k, v)


def _head_copies(k_hbm, v_hbm, kbuf, vbuf, sem, layer, stream):
    copies = []
    for h in range(kbuf.shape[1]):
        copies.append(pltpu.make_async_copy(
            k_hbm.at[layer, stream, :, h, :], kbuf.at[stream, h], sem.at[stream, 0]))
        copies.append(pltpu.make_async_copy(
            v_hbm.at[layer, stream, :, h, :], vbuf.at[stream, h], sem.at[stream, 1]))
    return copies


def _attn_sample_body(x_ref, gpre_ref, gpost_ref, wq_ref, wo_ref, k_hbm, v_hbm, o_ref,
                      kbuf, vbuf, sem, *, layer, rows):
    nb, nh = kbuf.shape[0], kbuf.shape[1]
    for s in range(nb):
        for c in _head_copies(k_hbm, v_hbm, kbuf, vbuf, sem, layer, s):
            c.start()
    x = x_ref[...]
    hd = x.shape[-1] // nh
    hn = _rms(x, gpre_ref[0]).astype(BF16)
    q = (_dot(hn, _w(wq_ref, 0)) * (hd ** -0.5)).astype(BF16)
    outs = []
    for s in range(nb):
        for c in _head_copies(k_hbm, v_hbm, kbuf, vbuf, sem, layer, s):
            c.wait()
        heads = []
        for h in range(nh):
            sc = _dot_nt(q[s * rows:(s + 1) * rows, h * hd:(h + 1) * hd], kbuf[s, h].astype(BF16))
            e = jnp.exp(sc - jnp.max(sc, axis=-1, keepdims=True))
            p = e / jnp.sum(e, axis=-1, keepdims=True)
            heads.append(_dot(p.astype(BF16), vbuf[s, h].astype(BF16)))
        outs.append(jnp.concatenate(heads, axis=-1))
    o = jnp.concatenate(outs, axis=0).astype(BF16)
    o_ref[...] = x + _rms(_dot(o, _w(wo_ref, 0)), gpost_ref[0])


def _attn_sample(x, npre, npost, wq, wo, k, v, layer, row0, rows):
    nt, d = x.shape
    depth, nb, nm, nh, hd = k.shape
    nsub = npre.shape[0] // depth
    nidx = layer * nsub + 2
    tile = nb * rows
    assert row0 % tile == 0
    t0 = row0 // tile
    once = dict(pipeline_mode=pl.Buffered(1))
    return pl.pallas_call(
        functools.partial(_attn_sample_body, layer=layer, rows=rows),
        grid=(1,),
        in_specs=[
            pl.BlockSpec((tile, d), lambda i: (t0, 0)),
            pl.BlockSpec((1, 1, d), lambda i: (nidx, 0, 0)),
            pl.BlockSpec((1, 1, d), lambda i: (nidx, 0, 0)),
            pl.BlockSpec((1, d, d), lambda i: (layer, 0, 0), **once),
            pl.BlockSpec((1, d, d), lambda i: (layer, 0, 0), **once),
            pl.BlockSpec(memory_space=pl.ANY),
            pl.BlockSpec(memory_space=pl.ANY),
        ],
        out_specs=pl.BlockSpec((tile, d), lambda i: (t0, 0)),
        out_shape=jax.ShapeDtypeStruct(x.shape, x.dtype),
        scratch_shapes=[pltpu.VMEM((nb, nh, nm, hd), F32), pltpu.VMEM((nb, nh, nm, hd), F32),
                        pltpu.SemaphoreType.DMA((nb, 2))],
        input_output_aliases={0: 0},
        compiler_params=_params(),
        name="xattn_sample",
    )(x, npre, npost, wq, wo, k, v)


def _pool_body(x_ref, gpre_ref, gpost_ref, win_ref, wgrp_ref, bgrp_ref, scale_ref, hist_ref,
               o_ref, buf_ref, ext_ref, *, tiles_per_batch, pos0, valid_rows):
    i = pl.program_id(0)
    tt, d = x_ref.shape
    ng = len(POOL_WINDOWS)
    gw = d // ng
    first = (i % tiles_per_batch) == 0

    x = x_ref[...]
    hn = _rms(x, gpre_ref[0]).astype(BF16)
    u = _dot(hn, _w(win_ref, 0))

    @pl.when(first)
    def _():
        ext_ref[0:HIST_ROWS, :] = hist_ref[0, 0]

    @pl.when(jnp.logical_not(first))
    def _():
        ext_ref[0:HIST_ROWS, :] = ext_ref[tt:tt + HIST_ROWS, :]

    ext_ref[HIST_ROWS:HIST_ROWS + tt, :] = u

    tile_pos = pos0 + (i % tiles_per_batch) * tt
    pos = tile_pos + lax.broadcasted_iota(jnp.int32, (tt, 1), 0)
    level = ext_ref[...]
    parts = []
    for g, w in enumerate(POOL_WINDOWS):
        level = level + pltpu.roll(level, w // 2, 0)
        cnt = jnp.minimum(pos + 1, w).astype(F32)
        parts.append(level[HIST_ROWS:HIST_ROWS + tt, 0:gw] * (1.0 / cnt))
        if g + 1 < ng:
            level = level[:, gw:]
    pooled = (jnp.concatenate(parts, axis=-1) - u).astype(BF16)

    mixed = []
    for g in range(ng):
        sl = slice(g * gw, (g + 1) * gw)
        mixed.append(_dot(pooled[:, sl], _w(wgrp_ref, 0, g)) + bgrp_ref[0, :, sl])
    out = jnp.concatenate(mixed, axis=-1) * scale_ref[0]
    o_ref[...] = x + _rms(out, gpost_ref[0])

    @pl.when((i % tiles_per_batch) == tiles_per_batch - 1)
    def _():
        buf_ref[0] = u[valid_rows - POOL_HIST:valid_rows, :]


def _pool(x, npre, npost, win, wgrp, bgrp, scale, hist, layer, j, row0, rows_per_batch,
          valid_rows_last, pos0, tt):
    nt, d = x.shape
    nb = hist.shape[1]
    nidx = layer * 4 + 1
    ng = len(POOL_WINDOWS)
    gw = d // ng
    t0 = row0 // tt
    tpb = rows_per_batch // tt
    body = functools.partial(_pool_body, tiles_per_batch=tpb, pos0=pos0, valid_rows=valid_rows_last)
    return pl.pallas_call(
        body,
        grid=(nb * tpb,),
        in_specs=[
            pl.BlockSpec((tt, d), lambda i: (t0 + i, 0)),
            pl.BlockSpec((1, 1, d), lambda i: (nidx, 0, 0)),
            pl.BlockSpec((1, 1, d), lambda i: (nidx, 0, 0)),
            pl.BlockSpec((1, d, d), lambda i: (j, 0, 0)),
            pl.BlockSpec((1, ng, gw, gw), lambda i: (j, 0, 0, 0)),
            pl.BlockSpec((1, 1, d), lambda i: (j, 0, 0)),
            pl.BlockSpec((1, 1, d), lambda i: (j, 0, 0)),
            pl.BlockSpec((1, 1, HIST_ROWS, d), lambda i: (j, i // tpb, 0, 0)),
        ],
        out_specs=[
            pl.BlockSpec((tt, d), lambda i: (t0 + i, 0)),
            pl.BlockSpec((1, POOL_HIST, d), lambda i: (i // tpb, 0, 0)),
        ],
        out_shape=[
            jax.ShapeDtypeStruct(x.shape, x.dtype),
            jax.ShapeDtypeStruct((nb, POOL_HIST, d), F32),
        ],
        scratch_shapes=[pltpu.VMEM((HIST_ROWS + tt, d), F32)],
        input_output_aliases={0: 0},
        compiler_params=_params(),
        name="pool_mixer",
    )(x, npre, npost, win, wgrp, bgrp, scale, hist)


def _cmul(ar, ai, br, bi):
    return ar * br - ai * bi, ar * bi + ai * br


def _cpow(zr, zi, n):
    rr, ri = None, None
    while n:
        if n & 1:
            rr, ri = (zr, zi) if rr is None else _cmul(rr, ri, zr, zi)
        n >>= 1
        if n:
            zr, zi = _cmul(zr, zi, zr, zi)
    return rr, ri


def _slot_channel(slot, nc):
    return slot // 2 + (nc // 2) * (slot % 2)


def _s5_param_group(gi, are_ref, aim_ref, ldt_ref, btr_ref, bti_ref, cr_ref, ci_ref,
                    kmat_ref, wst_ref, cpow_ref, lam_ref, *, sample_len):
    ch = S5_CHUNK
    nc, p = btr_ref.shape[1], btr_ref.shape[2]
    nq = nc // 2
    dt = jnp.exp(ldt_ref[gi])
    lre, lim = are_ref[gi], aim_ref[gi]
    zr, zi = lre * dt, lim * dt
    mag = jnp.exp(zr)
    lbr, lbi = mag * jnp.cos(zi), mag * jnp.sin(zi)
    den = lre * lre + lim * lim
    qr = ((lbr - 1.0) * lre + lbi * lim) / den
    qi = (lbi * lre - (lbr - 1.0) * lim) / den
    bbr, bbi = _cmul(qr, qi, btr_ref[gi], bti_ref[gi])

    n8 = lax.broadcasted_iota(jnp.int32, (SUBLANES, 1), 0)

    def base(expo):
        m = jnp.exp(expo * zr)
        a = expo * zi
        return m * jnp.cos(a), m * jnp.sin(a)

    asc_r, asc_i = base(n8.astype(F32))
    rev_r, rev_i = base((SUBLANES - 1 - n8).astype(F32))
    rows = SUBLANES
    while rows < ch:
        sr, si = _cpow(lbr, lbi, rows)
        ur, ui = _cmul(asc_r, asc_i, sr, si)
        asc_r, asc_i = jnp.concatenate([asc_r, ur], axis=0), jnp.concatenate([asc_i, ui], axis=0)
        ur, ui = _cmul(rev_r, rev_i, sr, si)
        rev_r, rev_i = jnp.concatenate([ur, rev_r], axis=0), jnp.concatenate([ui, rev_i], axis=0)
        rows *= 2
    p1r, p1i = _cmul(asc_r, asc_i, lbr, lbi)
    p1_ri = jnp.concatenate([p1r, p1i], axis=1)
    p1_ir = jnp.concatenate([p1i, p1r], axis=1)

    dead = jnp.zeros((ch - sample_len, p), F32)
    rev2_r = jnp.concatenate(
        [rev_r, jnp.concatenate([rev_r[ch - sample_len:], dead], axis=0)], axis=1)
    rev2_i = jnp.concatenate(
        [rev_i, jnp.concatenate([rev_i[ch - sample_len:], dead], axis=0)], axis=1)
    bb2_r = jnp.concatenate([bbr, bbr], axis=1)
    bb2_i = jnp.concatenate([bbi, bbi], axis=1)
    cr, ci = cr_ref[gi], ci_ref[gi]
    c_pm = jnp.concatenate([cr, -cr], axis=1)
    c_mm = jnp.concatenate([-ci, -ci], axis=1)
    w_r = [[], []]
    w_i = [[], []]
    for slot in range(nc):
        c = _slot_channel(slot, nc)
        wr, wi = _cmul(rev2_r, rev2_i, bb2_r[c:c + 1, :], bb2_i[c:c + 1, :])
        wst_ref[gi, slot * ch:(slot + 1) * ch, :] = jnp.concatenate([wr, wi], axis=1).astype(BF16)
        cpow_ref[gi, slot * ch:(slot + 1) * ch, :] = (
            p1_ri * c_pm[c:c + 1, :] + p1_ir * c_mm[c:c + 1, :]).astype(BF16)
        for h in range(2):
            r_, i_ = _cmul(cr[h * nq:(h + 1) * nq, :], ci[h * nq:(h + 1) * nq, :],
                           bbr[c:c + 1, :], bbi[c:c + 1, :])
            w_r[h].append(r_)
            w_i[h].append(i_)

    hi = lax.Precision.HIGHEST
    halves = [_dot_nt(jnp.concatenate(w_r[h], axis=0), asc_r, hi)
              - _dot_nt(jnp.concatenate(w_i[h], axis=0), asc_i, hi) for h in range(2)]
    kmat_ref[gi] = jnp.concatenate(halves, axis=1)

    llr, lli = _cpow(lbr, lbi, ch)
    lsr, lsi = _cpow(lbr, lbi, sample_len)
    lam = jnp.concatenate([llr, lli, lsr, lsi], axis=0)
    lam_ref[gi] = jnp.concatenate([lam, lam], axis=1)


def _s5_param_body(*refs, groups, sample_len):
    for gi in range(groups):
        _s5_param_group(gi, *refs, sample_len=sample_len)


def _s5_params(a_re, a_im, log_dt, b_re, b_im, c_re, c_im, sample_len):
    na, g, p, nc = b_re.shape
    n = na * g
    ch = S5_CHUNK
    gps = S5_GROUPS_PER_STEP
    assert n % gps == 0
    bt = lambda a: jnp.swapaxes(a, 2, 3).reshape(n, nc, p)
    spec = lambda *shape: pl.BlockSpec((gps,) + shape, lambda i: (i,) + (0,) * len(shape))
    return pl.pallas_call(
        functools.partial(_s5_param_body, groups=gps, sample_len=sample_len),
        grid=(n // gps,),
        in_specs=[spec(1, p), spec(1, p), spec(1, 1),
                  spec(nc, p), spec(nc, p), spec(nc, p), spec(nc, p)],
        out_specs=[spec(nc * nc // 2, LANES), spec(nc * ch, 4 * p), spec(nc * ch, 2 * p),
                   spec(4, 2 * p)],
        out_shape=[
            jax.ShapeDtypeStruct((n, nc * nc // 2, LANES), F32),
            jax.ShapeDtypeStruct((n, nc * ch, 4 * p), BF16),
            jax.ShapeDtypeStruct((n, nc * ch, 2 * p), BF16),
            jax.ShapeDtypeStruct((n, 4, 2 * p), F32),
        ],
        compiler_params=_params(),
        name="s5_params",
    )(a_re.reshape(n, 1, p), a_im.reshape(n, 1, p), log_dt.reshape(n, 1, 1),
      bt(b_re), bt(b_im), c_re.reshape(n, nc, p), c_im.reshape(n, nc, p))


def _lane_is_low(shape):
    return lax.broadcasted_iota(jnp.int32, shape, len(shape) - 1) < S5_CHUNK


def _s5_in_body(x_ref, gpre_ref, wt_ref, u_ref, *, nc):
    tt = x_ref.shape[0]
    nq = nc // 2
    hn = _rms(x_ref[...], gpre_ref[0]).astype(BF16)
    ut = _dot_nt(wt_ref[0], hn)
    ut = ut.reshape(ut.shape[0] // nc, nc, tt)
    low = _lane_is_low((ut.shape[0], nq, LANES))
    for m in range(tt // LANES):
        a = ut[:, 0:nq, m * LANES:(m + 1) * LANES]
        b = ut[:, nq:nc, m * LANES:(m + 1) * LANES]
        u_ref[:, (2 * m) * nq:(2 * m + 1) * nq, :] = jnp.where(low, a, pltpu.roll(b, S5_CHUNK, 2))
        u_ref[:, (2 * m + 1) * nq:(2 * m + 2) * nq, :] = jnp.where(low, pltpu.roll(a, S5_CHUNK, 2), b)


def _s5_in(x, npre, wt, layer, j, nc, tt):
    nt, d = x.shape
    dm = wt.shape[1]
    g = dm // nc
    nidx = layer * 4 + 1
    rows_per_tile = tt // S5_CHUNK * (nc // 2)
    return pl.pallas_call(
        functools.partial(_s5_in_body, nc=nc),
        grid=(nt // tt,),
        in_specs=[
            pl.BlockSpec((tt, d), lambda i: (i, 0)),
            pl.BlockSpec((1, 1, d), lambda i: (nidx, 0, 0)),
            pl.BlockSpec((1, dm, d), lambda i: (j, 0, 0)),
        ],
        out_specs=pl.BlockSpec((g, rows_per_tile, LANES), lambda i: (0, i, 0)),
        out_shape=jax.ShapeDtypeStruct((g, nt // S5_CHUNK * (nc // 2), LANES), F32),
        compiler_params=_params(),
        name="s5_in",
    )(x, npre, wt)


def _s5_group(gi, u_ref, kmat_ref, wst_ref, cpow_ref, lam_ref, dl_ref, h0r_ref, h0i_ref,
              z_ref, hpr_ref, hpi_ref, hsr_ref, hsi_ref, t_ref, *, nc, n_prompt_rows,
              rows_per_stream, n_streams):
    ch = S5_CHUNK
    nq = nc // 2
    nrows = u_ref.shape[1] // nq
    p = h0r_ref.shape[2]
    npr = n_prompt_rows
    rps = rows_per_stream

    row_i = lax.broadcasted_iota(jnp.int32, (ch, LANES), 0)
    col_i = lax.broadcasted_iota(jnp.int32, (ch, LANES), 1)
    causal = (col_i % ch) >= row_i
    for slot in range(nc):
        kr = kmat_ref[gi, slot * nq:(slot + 1) * nq, :]
        for q in range(nq):
            blk = jnp.broadcast_to(kr[q:q + 1, :], (ch, LANES))
            blk = pltpu.roll(blk, 0, 1, stride=1, stride_axis=0)
            t_ref[gi, slot * ch:(slot + 1) * ch, q * LANES:(q + 1) * LANES] = (
                jnp.where(causal, blk, 0.0).astype(BF16))

    lhs = jnp.concatenate(
        [u_ref[gi, pl.ds(q, nrows, stride=nq), :] for q in range(nq)], axis=1)
    lhs_b = lhs.astype(BF16)

    y = _dot(lhs_b, t_ref[gi])
    hl = _dot(lhs_b, wst_ref[gi])
    hl_r, hl_i = hl[:, 0:2 * p], hl[:, 2 * p:4 * p]

    lam = lam_ref[gi]
    hr, hi = hl_r[0:npr], hl_i[0:npr]
    k_in_stream = lax.broadcasted_iota(jnp.int32, (npr, 1), 0) % rps
    ar, ai = lam[0:1, :], lam[1:2, :]
    dist = 1
    while dist < rps:
        sr = jnp.where(k_in_stream >= dist, pltpu.roll(hr, dist, 0), 0.0)
        si = jnp.where(k_in_stream >= dist, pltpu.roll(hi, dist, 0), 0.0)
        mr, mi = _cmul(ar, ai, sr, si)
        hr, hi = hr + mr, hi + mi
        ar, ai = _cmul(ar, ai, ar, ai)
        dist *= 2
    for b in range(n_streams):
        last = (b + 1) * rps - 1
        hpr_ref[gi, b:b + 1, :] = hr[last:last + 1, 0:p]
        hpi_ref[gi, b:b + 1, :] = hi[last:last + 1, 0:p]
    ser = jnp.where(k_in_stream >= 1, pltpu.roll(hr, 1, 0), 0.0)[:, 0:p]
    sei = jnp.where(k_in_stream >= 1, pltpu.roll(hi, 1, 0), 0.0)[:, 0:p]

    h0r, h0i = h0r_ref[gi], h0i_ref[gi]
    fr, fi = _cmul(lam[2:3, 0:p], lam[3:4, 0:p], h0r, h0i)
    hsr_ref[gi] = fr + hl_r[npr:, p:2 * p]
    hsi_ref[gi] = fi + hl_i[npr:, p:2 * p]

    s_all = jnp.concatenate([jnp.concatenate([ser, sei], axis=1),
                             jnp.concatenate([h0r, h0i], axis=1)], axis=0).astype(BF16)
    y = y + _dot_nt(s_all, cpow_ref[gi])

    z = jax.nn.gelu(y + dl_ref[gi] * lhs)
    for q in range(nq):
        z_ref[gi, pl.ds(q, nrows, stride=nq), :] = z[:, q * LANES:(q + 1) * LANES]


def _s5_scan_body(*refs, groups, **static):
    for gi in range(groups):
        _s5_group(gi, *refs, **static)


def _s5_scan(u, kmat, wst, cpow, lam, dl, h0r, h0i, j, nc, n_prompt_rows, rows_per_stream,
             n_streams):
    ch = S5_CHUNK
    g, blk, _ = u.shape
    gps = S5_GROUPS_PER_STEP
    assert g % gps == 0
    nrows = blk // (nc // 2)
    p = h0r.shape[2]
    nsr = nrows - n_prompt_rows
    body = functools.partial(_s5_scan_body, groups=gps, nc=nc, n_prompt_rows=n_prompt_rows,
                             rows_per_stream=rows_per_stream, n_streams=n_streams)
    j0 = j * (g // gps)
    par = lambda *shape: pl.BlockSpec((gps,) + shape, lambda i: (j0 + i,) + (0,) * len(shape))
    act = lambda *shape: pl.BlockSpec((gps,) + shape, lambda i: (i,) + (0,) * len(shape))
    return pl.pallas_call(
        body,
        grid=(g // gps,),
        in_specs=[
            act(blk, LANES),
            par(nc * nc // 2, LANES),
            par(nc * ch, 4 * p),
            par(nc * ch, 2 * p),
            par(4, 2 * p),
            par(1, nc * ch),
            par(nsr, p),
            par(nsr, p),
        ],
        out_specs=[act(blk, LANES), act(n_streams, p), act(n_streams, p), act(nsr, p), act(nsr, p)],
        out_shape=[
            jax.ShapeDtypeStruct(u.shape, F32),
            jax.ShapeDtypeStruct((g, n_streams, p), F32),
            jax.ShapeDtypeStruct((g, n_streams, p), F32),
            jax.ShapeDtypeStruct((g, nsr, p), F32),
            jax.ShapeDtypeStruct((g, nsr, p), F32),
        ],
        scratch_shapes=[pltpu.VMEM((gps, nc * ch, nc * ch), BF16)],
        compiler_params=_params(),
        name="s5_scan",
    )(u, kmat, wst, cpow, lam, dl, h0r, h0i)


def _s5_out_body(x_ref, z_ref, gpost_ref, wglu_ref, o_ref, *, nc):
    tt, d = x_ref.shape
    g = z_ref.shape[0]
    nq = nc // 2
    low = _lane_is_low((g, nq, LANES))
    blocks = []
    for m in range(tt // LANES):
        e = z_ref[:, (2 * m) * nq:(2 * m + 1) * nq, :]
        o = z_ref[:, (2 * m + 1) * nq:(2 * m + 2) * nq, :]
        a = jnp.where(low, e, pltpu.roll(o, S5_CHUNK, 2))
        b = jnp.where(low, pltpu.roll(e, S5_CHUNK, 2), o)
        blocks.append(jnp.concatenate([a, b], axis=1).reshape(g * nc, LANES))
    zt = jnp.concatenate(blocks, axis=1)
    gate = lax.dot_general(zt.astype(BF16), _w(wglu_ref, 0), _TN, preferred_element_type=F32)
    out = gate[:, :d] * jax.nn.sigmoid(gate[:, d:])
    o_ref[...] = x_ref[...] + _rms(out, gpost_ref[0])


def _s5_out(x, z, npost, wglu, layer, j, nc, tt):
    nt, d = x.shape
    dm = wglu.shape[1]
    g = dm // nc
    nidx = layer * 4 + 1
    rows_per_tile = tt // S5_CHUNK * (nc // 2)
    return pl.pallas_call(
        functools.partial(_s5_out_body, nc=nc),
        grid=(nt // tt,),
        in_specs=[
            pl.BlockSpec((tt, d), lambda i: (i, 0)),
            pl.BlockSpec((g, rows_per_tile, LANES), lambda i: (0, i, 0)),
            pl.BlockSpec((1, 1, d), lambda i: (nidx, 0, 0)),
            pl.BlockSpec((1, dm, 2 * d), lambda i: (j, 0, 0)),
        ],
        out_specs=pl.BlockSpec((tt, d), lambda i: (i, 0)),
        out_shape=jax.ShapeDtypeStruct(x.shape, x.dtype),
        compiler_params=_params(),
        name="s5_out",
    )(x, z, npost, wglu)


def kernel(x_prompt, x_sample, cache_mem_k, cache_mem_v, state_ssm_re, state_ssm_im, state_pool,
           mem_prompt, norm_pre, norm_post, norm_mem, ffn_w_gate, ffn_w_up, ffn_w_down,
           xa_w_q, xa_w_k, xa_w_v, xa_w_o,
           s5_w_in, s5_a_re, s5_a_im, s5_b_re, s5_b_im, s5_c_re, s5_c_im, s5_d, s5_log_dt, s5_w_glu,
           pool_w_in, pool_w_grp, pool_b_grp, pool_scale):
    b, seq, d = x_prompt.shape
    db, ds, _ = x_sample.shape
    depth, nsub, _ = norm_pre.shape
    nm, nh = cache_mem_k.shape[2], cache_mem_k.shape[3]
    na, g, p, nc = s5_b_re.shape
    nb_layers = pool_w_in.shape[0]
    f = ffn_w_gate.shape[-1]
    ch = S5_CHUNK
    assert ds <= ch and ds % SUBLANES == 0 and ds >= POOL_HIST and PAST_LEN >= POOL_HIST
    assert nc == 2 * SUBLANES and g * nc == d and 2 * p == LANES

    n_prompt = b * seq
    nt = n_prompt + db * ch
    tm = min(512, seq)
    assert seq % tm == 0 and tm % LANES == 0 and nt % tm == 0

    wg = ffn_w_gate.reshape(depth * 2, d, f)
    wu = ffn_w_up.reshape(depth * 2, d, f)
    wd = ffn_w_down.reshape(depth * 2, f, d)
    wq, wk, wv, wo = xa_w_q, xa_w_k, xa_w_v, xa_w_o
    s5_wt = jnp.swapaxes(s5_w_in, 1, 2).astype(BF16)
    s5_glu = s5_w_glu
    p_win, p_wgrp = pool_w_in, pool_w_grp
    npre = norm_pre.reshape(depth * nsub, 1, d)
    npost = norm_post.reshape(depth * nsub, 1, d)

    assert db * ch == tm
    npt = n_prompt // tm
    xs = (x_prompt.reshape(n_prompt, d),
          jnp.pad(x_sample, ((0, 0), (0, ch - ds), (0, 0))).reshape(db * ch, d))

    mem_k_p, mem_v_p, kb_p, vb_p = _mem_kv(mem_prompt, norm_mem.reshape(depth, 1, d), wk, wv, nh)

    kmat, wst, cpow, lam = _s5_params(s5_a_re, s5_a_im, s5_log_dt, s5_b_re, s5_b_im,
                                      s5_c_re, s5_c_im, ds)
    slot_channel = [_slot_channel(slot, nc) for slot in range(nc)]
    dl = jnp.repeat(s5_d.reshape(na * g, 1, nc)[:, :, slot_channel], ch, axis=2)
    h0r = jnp.swapaxes(state_ssm_re, 1, 2).reshape(na * g, db, p)
    h0i = jnp.swapaxes(state_ssm_im, 1, 2).reshape(na * g, db, p)

    hist_p = jnp.zeros((nb_layers, b, HIST_ROWS, d), F32)
    hist_s = jnp.pad(state_pool, ((0, 0), (0, 0), (HIST_ROWS - POOL_HIST, 0), (0, 0)))
    pb = jnp.reshape(pool_b_grp, (nb_layers, 1, d))
    ps = jnp.reshape(pool_scale, (nb_layers, 1, d))

    ssm_p_re, ssm_p_im, ssm_s_re, ssm_s_im, buf_p, buf_s = [], [], [], [], [], []
    for i in range(depth):
        j = i // 2
        x = _ffn(xs if i == 0 else (x,), npre, npost, wg, wu, wd, i, 0, tm, npt)
        if i % 2 == 0:
            u = _s5_in(x, npre, s5_wt, i, j, nc, tm)
            z, hpr, hpi, hsr, hsi = _s5_scan(
                u, kmat, wst, cpow, lam, dl, h0r, h0i, j, nc, n_prompt // ch, seq // ch, b)
            x = _s5_out(x, z, npost, s5_glu, i, j, nc, tm)
            ssm_p_re.append(jnp.swapaxes(hpr, 0, 1))
            ssm_p_im.append(jnp.swapaxes(hpi, 0, 1))
            ssm_s_re.append(jnp.swapaxes(hsr, 0, 1))
            ssm_s_im.append(jnp.swapaxes(hsi, 0, 1))
        else:
            tp = min(POOL_TILE, seq)
            x, bp = _pool(x, npre, npost, p_win, p_wgrp, pb, ps, hist_p, i, j, 0, seq, tp, 0, tp)
            x, bs = _pool(x, npre, npost, p_win, p_wgrp, pb, ps, hist_s, i, j, n_prompt, ch, ds,
                          PAST_LEN, ch)
            buf_p.append(bp)
            buf_s.append(bs)
        x = _attn(x, npre, npost, wq, wo, kb_p, vb_p, i, nh, seq, min(ATTN_TILE, seq))
        x = _attn_sample(x, npre, npost, wq, wo, cache_mem_k, cache_mem_v, i, n_prompt, ch)
        x = _ffn((x,), npre, npost, wg, wu, wd, i, 1, tm, npt, split_out=(i == depth - 1))

    y_prompt = x[0].reshape(b, seq, d)
    y_sample = x[1].reshape(db, ch, d)[:, :ds, :]
    return (y_prompt, y_sample, mem_k_p, mem_v_p,
            jnp.stack(ssm_p_re), jnp.stack(ssm_p_im), jnp.stack(buf_p),
            jnp.stack(ssm_s_re), jnp.stack(ssm_s_im), jnp.stack(buf_s))
```

```python
import functools

import jax
import jax.numpy as jnp
from jax import lax
from jax.experimental import pallas as pl
from jax.experimental.pallas import tpu as pltpu

F32 = jnp.float32
BF16 = jnp.bfloat16

RMS_EPS = 1e-6
PAST_LEN = 1024
POOL_WINDOWS = (2, 4, 8, 16)
POOL_HIST = max(POOL_WINDOWS) - 1
HIST_ROWS = 16

LANES = 128
SUBLANES = 8
S5_CHUNK = LANES // 2
S5_GROUPS_PER_STEP = 2
S5_PARAM_GROUPS_PER_STEP = 4
FFN_ROW_CHUNK = 256
ATTN_TILE = 1024
POOL_TILE = 1024
VMEM_LIMIT = 56 * 1024 * 1024

_NT = (((1,), (1,)), ((), ()))
_TN = (((0,), (0,)), ((), ()))


def _params(n_axes=1, vmem=VMEM_LIMIT):
    return pltpu.CompilerParams(
        dimension_semantics=("arbitrary",) * n_axes, vmem_limit_bytes=vmem)


def _rms(x, g):
    ms = jnp.mean(x * x, axis=-1, keepdims=True)
    return x * lax.rsqrt(ms + RMS_EPS) * g


def _dot(a, b):
    return jnp.dot(a, b, preferred_element_type=F32)


def _w(ref, *idx):
    return ref[idx].astype(BF16)


def _dot_nt(a, b, precision=None):
    return lax.dot_general(a, b, _NT, preferred_element_type=F32, precision=precision)


def _ffn_body(*refs, n_in, n_out, n_next, n_prompt_tiles, row_chunk):
    x_refs, (gpre_ref, gpost_ref, wg_ref, wu_ref, wd_ref) = refs[:n_in], refs[n_in:n_in + 5]
    next_refs = refs[n_in + 5:n_in + 5 + n_next]
    o_refs = refs[n_in + 5 + n_next:n_in + 5 + n_next + n_out]
    rounded_refs = refs[n_in + 5 + n_next + n_out:]
    i = pl.program_id(0)
    tm = x_refs[0].shape[0]
    on_prompt = i < n_prompt_tiles
    for r in range(tm // row_chunk):
        rows = slice(r * row_chunk, (r + 1) * row_chunk)
        x = x_refs[0][rows, :]
        if n_in == 2:
            x = jnp.where(on_prompt, x, x_refs[1][rows, :])
        hn = _rms(x, gpre_ref[0]).astype(BF16)
        g = _dot(hn, _w(wg_ref, 0))
        u = _dot(hn, _w(wu_ref, 0))
        a = (g * jax.nn.sigmoid(g) * u).astype(BF16)
        o_refs[-1][rows, :] = x + 0.5 * _rms(_dot(a, _w(wd_ref, 0)), gpost_ref[0])
    if n_out == 2:
        @pl.when(on_prompt)
        def _():
            o_refs[0][...] = o_refs[1][...]
    for src, dst in zip(next_refs, rounded_refs):
        dst[...] = src[...].astype(BF16)


def _slab_rows(rows, n_steps):
    need = -(-rows // n_steps)
    return min(r for r in range(16, rows + 1, 16) if rows % r == 0 and r >= need)


def _ffn(xs, npre, npost, w, widx, nidx, tm, n_prompt_tiles, split_out=False, w_next=None):
    d = xs[0].shape[1]
    f = w[0].shape[-1]
    last_prompt = n_prompt_tiles - 1
    joint = pl.BlockSpec((tm, d), lambda i: (i, 0))
    prompt = pl.BlockSpec((tm, d), lambda i: (jnp.minimum(i, last_prompt), 0))
    sample = pl.BlockSpec((tm, d), lambda i: (0, 0))
    once = dict(pipeline_mode=pl.Buffered(1))
    if len(xs) == 2:
        assert xs[0].shape[0] == n_prompt_tiles * tm and xs[1].shape[0] == tm
    n_steps = n_prompt_tiles + 1
    nt = n_steps * tm
    if split_out:
        out_specs = [prompt, sample]
        out_shape = [jax.ShapeDtypeStruct((n_prompt_tiles * tm, d), F32),
                     jax.ShapeDtypeStruct((tm, d), F32)]
    else:
        out_specs = [joint]
        out_shape = [jax.ShapeDtypeStruct((nt, d), F32)]
    next_args, next_specs = (), []
    if w_next is not None:
        stacks, nxt = w_next
        for a in stacks:
            rows, cols = a.shape[1], a.shape[2]
            slab = _slab_rows(rows, n_steps)
            last = rows // slab - 1
            next_args += (a,)
            next_specs.append(pl.BlockSpec(
                (1, slab, cols), lambda i, last=last: (nxt, jnp.minimum(i, last), 0)))
            out_specs.append(pl.BlockSpec(
                (1, slab, cols), lambda i, last=last: (0, jnp.minimum(i, last), 0)))
            out_shape.append(jax.ShapeDtypeStruct((1, rows, cols), BF16))
    body = functools.partial(_ffn_body, n_in=len(xs), n_out=2 if split_out else 1,
                             n_next=len(next_args), n_prompt_tiles=n_prompt_tiles,
                             row_chunk=FFN_ROW_CHUNK)
    outs = pl.pallas_call(
        body,
        grid=(n_steps,),
        in_specs=([joint] if len(xs) == 1 else [prompt, sample]) + [
            pl.BlockSpec((1, 1, d), lambda i: (nidx, 0, 0)),
            pl.BlockSpec((1, 1, d), lambda i: (nidx, 0, 0)),
            pl.BlockSpec((1, d, f), lambda i: (widx, 0, 0), **once),
            pl.BlockSpec((1, d, f), lambda i: (widx, 0, 0), **once),
            pl.BlockSpec((1, f, d), lambda i: (widx, 0, 0), **once),
        ] + next_specs,
        out_specs=out_specs,
        out_shape=out_shape,
        compiler_params=_params(),
        name="ffn",
    )(*xs, npre, npost, *w, *next_args)
    n_y = 2 if split_out else 1
    y = tuple(outs[:n_y]) if split_out else outs[0]
    return y, (tuple(outs[n_y:]) if w_next is not None else None)


def _kv_body(m_ref, g_ref, wk_ref, wv_ref, k_ref, v_ref, kb_ref, vb_ref):
    nh, hd = k_ref.shape[3], k_ref.shape[4]
    m = _rms(m_ref[0], g_ref[0]).astype(BF16)
    for w_ref, out_ref, bf_ref in ((wk_ref, k_ref, kb_ref), (wv_ref, v_ref, vb_ref)):
        y = _dot(m, _w(w_ref, 0))
        bf_ref[0, 0] = y.astype(BF16)
        for h in range(nh):
            out_ref[0, 0, :, h, :] = y[:, h * hd:(h + 1) * hd]


def _mem_kv(mem, nmem, wk, wv, nh):
    b, nm, d = mem.shape
    depth = wk.shape[0]
    hd = d // nh
    out = jax.ShapeDtypeStruct((depth, b, nm, nh, hd), F32)
    out_b = jax.ShapeDtypeStruct((depth, b, nm, d), BF16)
    return pl.pallas_call(
        _kv_body,
        grid=(depth, b),
        in_specs=[
            pl.BlockSpec((1, nm, d), lambda l, j: (j, 0, 0)),
            pl.BlockSpec((1, 1, d), lambda l, j: (l, 0, 0)),
            pl.BlockSpec((1, d, d), lambda l, j: (l, 0, 0)),
            pl.BlockSpec((1, d, d), lambda l, j: (l, 0, 0)),
        ],
        out_specs=[pl.BlockSpec((1, 1, nm, nh, hd), lambda l, j: (l, j, 0, 0, 0))] * 2
        + [pl.BlockSpec((1, 1, nm, d), lambda l, j: (l, j, 0, 0))] * 2,
        out_shape=[out, out, out_b, out_b],
        compiler_params=_params(2),
        name="mem_kv",
    )(mem, nmem, wk, wv)


def _attn_rows(x, gpre, gpost, wq, wo, kv, nh):
    hd = x.shape[-1] // nh
    hn = _rms(x, gpre).astype(BF16)
    q = (_dot(hn, wq) * (hd ** -0.5)).astype(BF16)
    heads = []
    for h, (kh, vh) in enumerate(kv):
        s = _dot_nt(q[:, h * hd:(h + 1) * hd], kh)
        e = jnp.exp(s - jnp.max(s, axis=-1, keepdims=True))
        p = e / jnp.sum(e, axis=-1, keepdims=True)
        heads.append(_dot(p.astype(BF16), vh))
    o = jnp.concatenate(heads, axis=-1).astype(BF16)
    return x + _rms(_dot(o, wo), gpost)


def _attn_body(x_ref, gpre_ref, gpost_ref, wq_ref, wo_ref, k_ref, v_ref, o_ref, *, nh):
    hd = x_ref.shape[1] // nh
    kv = [(k_ref[0, 0, :, h * hd:(h + 1) * hd], v_ref[0, 0, :, h * hd:(h + 1) * hd])
          for h in range(nh)]
    o_ref[...] = _attn_rows(x_ref[...], gpre_ref[0], gpost_ref[0], _w(wq_ref, 0), _w(wo_ref, 0),
                            kv, nh)


def _attn(x, npre, npost, wq, wo, k, v, layer, nh, rows_per_batch, ta):
    nt, d = x.shape
    depth, nb, nm, _ = k.shape
    nsub = npre.shape[0] // depth
    nidx = layer * nsub + 2
    tpb = rows_per_batch // ta
    kv_spec = pl.BlockSpec((1, 1, nm, d), lambda i: (layer, i // tpb, 0, 0))
    return pl.pallas_call(
        functools.partial(_attn_body, nh=nh),
        grid=(nb * tpb,),
        in_specs=[
            pl.BlockSpec((ta, d), lambda i: (i, 0)),
            pl.BlockSpec((1, 1, d), lambda i: (nidx, 0, 0)),
            pl.BlockSpec((1, 1, d), lambda i: (nidx, 0, 0)),
            pl.BlockSpec((1, d, d), lambda i: (layer, 0, 0), pipeline_mode=pl.Buffered(1)),
            pl.BlockSpec((1, d, d), lambda i: (layer, 0, 0), pipeline_mode=pl.Buffered(1)),
            kv_spec,
            kv_spec,
        ],
        out_specs=pl.BlockSpec((ta, d), lambda i: (i, 0)),
        out_shape=jax.ShapeDtypeStruct(x.shape, x.dtype),
        input_output_aliases={0: 0},
        compiler_params=_params(),
        name="xattn",
    )(x, npre, npost, wq, wo, k, v)


def _head_copies(k_hbm, v_hbm, kbuf, vbuf, sem, layer, stream):
    copies = []
    for h in range(kbuf.shape[1]):
        copies.append(pltpu.make_async_copy(
            k_hbm.at[layer, stream, :, h, :], kbuf.at[stream, h], sem.at[stream, 0]))
        copies.append(pltpu.make_async_copy(
            v_hbm.at[layer, stream, :, h, :], vbuf.at[stream, h], sem.at[stream, 1]))
    return copies


def _attn_sample_body(x_ref, gpre_ref, gpost_ref, wq_ref, wo_ref, k_hbm, v_hbm, o_ref,
                      kbuf, vbuf, sem, *, layer, rows):
    nb, nh = kbuf.shape[0], kbuf.shape[1]
    for s in range(nb):
        for c in _head_copies(k_hbm, v_hbm, kbuf, vbuf, sem, layer, s):
            c.start()
    x = x_ref[...]
    hd = x.shape[-1] // nh
    hn = _rms(x, gpre_ref[0]).astype(BF16)
    q = (_dot(hn, _w(wq_ref, 0)) * (hd ** -0.5)).astype(BF16)
    outs = []
    for s in range(nb):
        for c in _head_copies(k_hbm, v_hbm, kbuf, vbuf, sem, layer, s):
            c.wait()
        heads = []
        for h in range(nh):
            sc = _dot_nt(q[s * rows:(s + 1) * rows, h * hd:(h + 1) * hd], kbuf[s, h].astype(BF16))
            e = jnp.exp(sc - jnp.max(sc, axis=-1, keepdims=True))
            p = e / jnp.sum(e, axis=-1, keepdims=True)
            heads.append(_dot(p.astype(BF16), vbuf[s, h].astype(BF16)))
        outs.append(jnp.concatenate(heads, axis=-1))
    o = jnp.concatenate(outs, axis=0).astype(BF16)
    o_ref[...] = x + _rms(_dot(o, _w(wo_ref, 0)), gpost_ref[0])


def _attn_sample(x, npre, npost, wq, wo, k, v, layer, row0, rows):
    nt, d = x.shape
    depth, nb, nm, nh, hd = k.shape
    nsub = npre.shape[0] // depth
    nidx = layer * nsub + 2
    tile = nb * rows
    assert row0 % tile == 0
    t0 = row0 // tile
    once = dict(pipeline_mode=pl.Buffered(1))
    return pl.pallas_call(
        functools.partial(_attn_sample_body, layer=layer, rows=rows),
        grid=(1,),
        in_specs=[
            pl.BlockSpec((tile, d), lambda i: (t0, 0)),
            pl.BlockSpec((1, 1, d), lambda i: (nidx, 0, 0)),
            pl.BlockSpec((1, 1, d), lambda i: (nidx, 0, 0)),
            pl.BlockSpec((1, d, d), lambda i: (layer, 0, 0), **once),
            pl.BlockSpec((1, d, d), lambda i: (layer, 0, 0), **once),
            pl.BlockSpec(memory_space=pl.ANY),
            pl.BlockSpec(memory_space=pl.ANY),
        ],
        out_specs=pl.BlockSpec((tile, d), lambda i: (t0, 0)),
        out_shape=jax.ShapeDtypeStruct(x.shape, x.dtype),
        scratch_shapes=[pltpu.VMEM((nb, nh, nm, hd), F32), pltpu.VMEM((nb, nh, nm, hd), F32),
                        pltpu.SemaphoreType.DMA((nb, 2))],
        input_output_aliases={0: 0},
        compiler_params=_params(),
        name="xattn_sample",
    )(x, npre, npost, wq, wo, k, v)


def _pool_body(x_ref, gpre_ref, gpost_ref, win_ref, wgrp_ref, bgrp_ref, scale_ref, hist_ref,
               o_ref, buf_ref, ext_ref, *, tiles_per_batch, pos0, valid_rows):
    i = pl.program_id(0)
    tt, d = x_ref.shape
    ng = len(POOL_WINDOWS)
    gw = d // ng
    first = (i % tiles_per_batch) == 0

    x = x_ref[...]
    hn = _rms(x, gpre_ref[0]).astype(BF16)
    u = _dot(hn, _w(win_ref, 0))

    @pl.when(first)
    def _():
        ext_ref[0:HIST_ROWS, :] = hist_ref[0, 0]

    @pl.when(jnp.logical_not(first))
    def _():
        ext_ref[0:HIST_ROWS, :] = ext_ref[tt:tt + HIST_ROWS, :]

    ext_ref[HIST_ROWS:HIST_ROWS + tt, :] = u

    tile_pos = pos0 + (i % tiles_per_batch) * tt
    pos = tile_pos + lax.broadcasted_iota(jnp.int32, (tt, 1), 0)
    level = ext_ref[...]
    parts = []
    for g, w in enumerate(POOL_WINDOWS):
        level = level + pltpu.roll(level, w // 2, 0)
        cnt = jnp.minimum(pos + 1, w).astype(F32)
        parts.append(level[HIST_ROWS:HIST_ROWS + tt, 0:gw] * (1.0 / cnt))
        if g + 1 < ng:
            level = level[:, gw:]
    pooled = (jnp.concatenate(parts, axis=-1) - u).astype(BF16)

    mixed = []
    for g in range(ng):
        sl = slice(g * gw, (g + 1) * gw)
        mixed.append(_dot(pooled[:, sl], _w(wgrp_ref, 0, g)) + bgrp_ref[0, :, sl])
    out = jnp.concatenate(mixed, axis=-1) * scale_ref[0]
    o_ref[...] = x + _rms(out, gpost_ref[0])

    @pl.when((i % tiles_per_batch) == tiles_per_batch - 1)
    def _():
        buf_ref[0] = u[valid_rows - POOL_HIST:valid_rows, :]


def _pool(x, npre, npost, win, wgrp, bgrp, scale, hist, layer, j, row0, rows_per_batch,
          valid_rows_last, pos0, tt):
    nt, d = x.shape
    nb = hist.shape[1]
    nidx = layer * 4 + 1
    ng = len(POOL_WINDOWS)
    gw = d // ng
    t0 = row0 // tt
    tpb = rows_per_batch // tt
    body = functools.partial(_pool_body, tiles_per_batch=tpb, pos0=pos0, valid_rows=valid_rows_last)
    return pl.pallas_call(
        body,
        grid=(nb * tpb,),
        in_specs=[
            pl.BlockSpec((tt, d), lambda i: (t0 + i, 0)),
            pl.BlockSpec((1, 1, d), lambda i: (nidx, 0, 0)),
            pl.BlockSpec((1, 1, d), lambda i: (nidx, 0, 0)),
            pl.BlockSpec((1, d, d), lambda i: (j, 0, 0)),
            pl.BlockSpec((1, ng, gw, gw), lambda i: (j, 0, 0, 0)),
            pl.BlockSpec((1, 1, d), lambda i: (j, 0, 0)),
            pl.BlockSpec((1, 1, d), lambda i: (j, 0, 0)),
            pl.BlockSpec((1, 1, HIST_ROWS, d), lambda i: (j, i // tpb, 0, 0)),
        ],
        out_specs=[
            pl.BlockSpec((tt, d), lambda i: (t0 + i, 0)),
            pl.BlockSpec((1, POOL_HIST, d), lambda i: (i // tpb, 0, 0)),
        ],
        out_shape=[
            jax.ShapeDtypeStruct(x.shape, x.dtype),
            jax.ShapeDtypeStruct((nb, POOL_HIST, d), F32),
        ],
        scratch_shapes=[pltpu.VMEM((HIST_ROWS + tt, d), F32)],
        input_output_aliases={0: 0},
        compiler_params=_params(),
        name="pool_mixer",
    )(x, npre, npost, win, wgrp, bgrp, scale, hist)


def _cmul(ar, ai, br, bi):
    return ar * br - ai * bi, ar * bi + ai * br


def _cpow(zr, zi, n):
    rr, ri = None, None
    while n:
        if n & 1:
            rr, ri = (zr, zi) if rr is None else _cmul(rr, ri, zr, zi)
        n >>= 1
        if n:
            zr, zi = _cmul(zr, zi, zr, zi)
    return rr, ri


def _slot_channel(slot, nc):
    return slot // 2 + (nc // 2) * (slot % 2)


def _s5_param_group(gi, are_ref, aim_ref, ldt_ref, btr_ref, bti_ref, cr_ref, ci_ref,
                    kmat_ref, wst_ref, cpow_ref, lam_ref, *, sample_len):
    ch = S5_CHUNK
    nc, p = btr_ref.shape[1], btr_ref.shape[2]
    nq = nc // 2
    dt = jnp.exp(ldt_ref[gi])
    lre, lim = are_ref[gi], aim_ref[gi]
    zr, zi = lre * dt, lim * dt
    mag = jnp.exp(zr)
    lbr, lbi = mag * jnp.cos(zi), mag * jnp.sin(zi)
    den = lre * lre + lim * lim
    qr = ((lbr - 1.0) * lre + lbi * lim) / den
    qi = (lbi * lre - (lbr - 1.0) * lim) / den
    bbr, bbi = _cmul(qr, qi, btr_ref[gi], bti_ref[gi])

    n8 = lax.broadcasted_iota(jnp.int32, (SUBLANES, 1), 0)

    def base(expo):
        m = jnp.exp(expo * zr)
        a = expo * zi
        return m * jnp.cos(a), m * jnp.sin(a)

    asc_r, asc_i = base(n8.astype(F32))
    rev_r, rev_i = base((SUBLANES - 1 - n8).astype(F32))
    rows = SUBLANES
    while rows < ch:
        sr, si = _cpow(lbr, lbi, rows)
        ur, ui = _cmul(asc_r, asc_i, sr, si)
        asc_r, asc_i = jnp.concatenate([asc_r, ur], axis=0), jnp.concatenate([asc_i, ui], axis=0)
        ur, ui = _cmul(rev_r, rev_i, sr, si)
        rev_r, rev_i = jnp.concatenate([ur, rev_r], axis=0), jnp.concatenate([ui, rev_i], axis=0)
        rows *= 2
    p1r, p1i = _cmul(asc_r, asc_i, lbr, lbi)
    p1_ri = jnp.concatenate([p1r, p1i], axis=1)
    p1_ir = jnp.concatenate([p1i, p1r], axis=1)

    dead = jnp.zeros((ch - sample_len, p), F32)
    rev2_r = jnp.concatenate(
        [rev_r, jnp.concatenate([rev_r[ch - sample_len:], dead], axis=0)], axis=1)
    rev2_i = jnp.concatenate(
        [rev_i, jnp.concatenate([rev_i[ch - sample_len:], dead], axis=0)], axis=1)
    bb2_r = jnp.concatenate([bbr, bbr], axis=1)
    bb2_i = jnp.concatenate([bbi, bbi], axis=1)
    cr, ci = cr_ref[gi], ci_ref[gi]
    c_pm = jnp.concatenate([cr, -cr], axis=1)
    c_mm = jnp.concatenate([-ci, -ci], axis=1)
    w_r = [[], []]
    w_i = [[], []]
    for slot in range(nc):
        c = _slot_channel(slot, nc)
        wr, wi = _cmul(rev2_r, rev2_i, bb2_r[c:c + 1, :], bb2_i[c:c + 1, :])
        wst_ref[gi, slot * ch:(slot + 1) * ch, :] = jnp.concatenate([wr, wi], axis=1).astype(BF16)
        cpow_ref[gi, slot * ch:(slot + 1) * ch, :] = (
            p1_ri * c_pm[c:c + 1, :] + p1_ir * c_mm[c:c + 1, :]).astype(BF16)
        for h in range(2):
            r_, i_ = _cmul(cr[h * nq:(h + 1) * nq, :], ci[h * nq:(h + 1) * nq, :],
                           bbr[c:c + 1, :], bbi[c:c + 1, :])
            w_r[h].append(r_)
            w_i[h].append(i_)

    hi = lax.Precision.HIGHEST
    halves = [_dot_nt(jnp.concatenate(w_r[h], axis=0), asc_r, hi)
              - _dot_nt(jnp.concatenate(w_i[h], axis=0), asc_i, hi) for h in range(2)]
    kmat_ref[gi] = jnp.concatenate(halves, axis=1)

    llr, lli = _cpow(lbr, lbi, ch)
    lsr, lsi = _cpow(lbr, lbi, sample_len)
    lam = jnp.concatenate([llr, lli, lsr, lsi], axis=0)
    lam_ref[gi] = jnp.concatenate([lam, lam], axis=1)


def _s5_param_body(*refs, groups, sample_len):
    for gi in range(groups):
        _s5_param_group(gi, *refs, sample_len=sample_len)


def _s5_params(a_re, a_im, log_dt, b_re, b_im, c_re, c_im, sample_len):
    na, g, p, nc = b_re.shape
    n = na * g
    ch = S5_CHUNK
    gps = S5_PARAM_GROUPS_PER_STEP
    assert n % gps == 0
    bt = lambda a: jnp.swapaxes(a, 2, 3).reshape(n, nc, p)
    spec = lambda *shape: pl.BlockSpec((gps,) + shape, lambda i: (i,) + (0,) * len(shape))
    return pl.pallas_call(
        functools.partial(_s5_param_body, groups=gps, sample_len=sample_len),
        grid=(n // gps,),
        in_specs=[spec(1, p), spec(1, p), spec(1, 1),
                  spec(nc, p), spec(nc, p), spec(nc, p), spec(nc, p)],
        out_specs=[spec(nc * nc // 2, LANES), spec(nc * ch, 4 * p), spec(nc * ch, 2 * p),
                   spec(4, 2 * p)],
        out_shape=[
            jax.ShapeDtypeStruct((n, nc * nc // 2, LANES), F32),
            jax.ShapeDtypeStruct((n, nc * ch, 4 * p), BF16),
            jax.ShapeDtypeStruct((n, nc * ch, 2 * p), BF16),
            jax.ShapeDtypeStruct((n, 4, 2 * p), F32),
        ],
        compiler_params=_params(),
        name="s5_params",
    )(a_re.reshape(n, 1, p), a_im.reshape(n, 1, p), log_dt.reshape(n, 1, 1),
      bt(b_re), bt(b_im), c_re.reshape(n, nc, p), c_im.reshape(n, nc, p))


def _lane_is_low(shape):
    return lax.broadcasted_iota(jnp.int32, shape, len(shape) - 1) < S5_CHUNK


def _s5_in_body(x_ref, gpre_ref, wt_ref, u_ref, *, nc):
    tt = x_ref.shape[0]
    nq = nc // 2
    hn = _rms(x_ref[...], gpre_ref[0]).astype(BF16)
    ut = _dot_nt(wt_ref[0], hn)
    ut = ut.reshape(ut.shape[0] // nc, nc, tt)
    low = _lane_is_low((ut.shape[0], nq, LANES))
    for m in range(tt // LANES):
        a = ut[:, 0:nq, m * LANES:(m + 1) * LANES]
        b = ut[:, nq:nc, m * LANES:(m + 1) * LANES]
        u_ref[:, (2 * m) * nq:(2 * m + 1) * nq, :] = jnp.where(low, a, pltpu.roll(b, S5_CHUNK, 2))
        u_ref[:, (2 * m + 1) * nq:(2 * m + 2) * nq, :] = jnp.where(low, pltpu.roll(a, S5_CHUNK, 2), b)


def _s5_in(x, npre, wt, layer, j, nc, tt):
    nt, d = x.shape
    dm = wt.shape[1]
    g = dm // nc
    nidx = layer * 4 + 1
    rows_per_tile = tt // S5_CHUNK * (nc // 2)
    return pl.pallas_call(
        functools.partial(_s5_in_body, nc=nc),
        grid=(nt // tt,),
        in_specs=[
            pl.BlockSpec((tt, d), lambda i: (i, 0)),
            pl.BlockSpec((1, 1, d), lambda i: (nidx, 0, 0)),
            pl.BlockSpec((1, dm, d), lambda i: (j, 0, 0)),
        ],
        out_specs=pl.BlockSpec((g, rows_per_tile, LANES), lambda i: (0, i, 0)),
        out_shape=jax.ShapeDtypeStruct((g, nt // S5_CHUNK * (nc // 2), LANES), F32),
        compiler_params=_params(),
        name="s5_in",
    )(x, npre, wt)


def _s5_group(gi, u_ref, kmat_ref, wst_ref, cpow_ref, lam_ref, dl_ref, h0r_ref, h0i_ref,
              z_ref, hpr_ref, hpi_ref, hsr_ref, hsi_ref, t_ref, *, nc, n_prompt_rows,
              rows_per_stream, n_streams):
    ch = S5_CHUNK
    nq = nc // 2
    nrows = u_ref.shape[1] // nq
    p = h0r_ref.shape[2]
    npr = n_prompt_rows
    rps = rows_per_stream

    row_i = lax.broadcasted_iota(jnp.int32, (ch, LANES), 0)
    col_i = lax.broadcasted_iota(jnp.int32, (ch, LANES), 1)
    causal = (col_i % ch) >= row_i
    for slot in range(nc):
        kr = kmat_ref[gi, slot * nq:(slot + 1) * nq, :]
        for q in range(nq):
            blk = jnp.broadcast_to(kr[q:q + 1, :], (ch, LANES))
            blk = pltpu.roll(blk, 0, 1, stride=1, stride_axis=0)
            t_ref[gi, slot * ch:(slot + 1) * ch, q * LANES:(q + 1) * LANES] = (
                jnp.where(causal, blk, 0.0).astype(BF16))

    lhs = jnp.concatenate(
        [u_ref[gi, pl.ds(q, nrows, stride=nq), :] for q in range(nq)], axis=1)
    lhs_b = lhs.astype(BF16)

    y = _dot(lhs_b, t_ref[gi])
    hl = _dot(lhs_b, wst_ref[gi])
    hl_r, hl_i = hl[:, 0:2 * p], hl[:, 2 * p:4 * p]

    lam = lam_ref[gi]
    hr, hi = hl_r[0:npr], hl_i[0:npr]
    k_in_stream = lax.broadcasted_iota(jnp.int32, (npr, 1), 0) % rps
    ar, ai = lam[0:1, :], lam[1:2, :]
    dist = 1
    while dist < rps:
        sr = jnp.where(k_in_stream >= dist, pltpu.roll(hr, dist, 0), 0.0)
        si = jnp.where(k_in_stream >= dist, pltpu.roll(hi, dist, 0), 0.0)
        mr, mi = _cmul(ar, ai, sr, si)
        hr, hi = hr + mr, hi + mi
        ar, ai = _cmul(ar, ai, ar, ai)
        dist *= 2
    for b in range(n_streams):
        last = (b + 1) * rps - 1
        hpr_ref[gi, b:b + 1, :] = hr[last:last + 1, 0:p]
        hpi_ref[gi, b:b + 1, :] = hi[last:last + 1, 0:p]
    ser = jnp.where(k_in_stream >= 1, pltpu.roll(hr, 1, 0), 0.0)[:, 0:p]
    sei = jnp.where(k_in_stream >= 1, pltpu.roll(hi, 1, 0), 0.0)[:, 0:p]

    h0r, h0i = h0r_ref[gi], h0i_ref[gi]
    fr, fi = _cmul(lam[2:3, 0:p], lam[3:4, 0:p], h0r, h0i)
    hsr_ref[gi] = fr + hl_r[npr:, p:2 * p]
    hsi_ref[gi] = fi + hl_i[npr:, p:2 * p]

    s_all = jnp.concatenate([jnp.concatenate([ser, sei], axis=1),
                             jnp.concatenate([h0r, h0i], axis=1)], axis=0).astype(BF16)
    y = y + _dot_nt(s_all, cpow_ref[gi])

    z = jax.nn.gelu(y + dl_ref[gi] * lhs)
    for q in range(nq):
        z_ref[gi, pl.ds(q, nrows, stride=nq), :] = z[:, q * LANES:(q + 1) * LANES]


def _s5_scan_body(*refs, groups, **static):
    for gi in range(groups):
        _s5_group(gi, *refs, **static)


def _s5_scan(u, kmat, wst, cpow, lam, dl, h0r, h0i, j, nc, n_prompt_rows, rows_per_stream,
             n_streams):
    ch = S5_CHUNK
    g, blk, _ = u.shape
    gps = S5_GROUPS_PER_STEP
    assert g % gps == 0
    nrows = blk // (nc // 2)
    p = h0r.shape[2]
    nsr = nrows - n_prompt_rows
    body = functools.partial(_s5_scan_body, groups=gps, nc=nc, n_prompt_rows=n_prompt_rows,
                             rows_per_stream=rows_per_stream, n_streams=n_streams)
    j0 = j * (g // gps)
    par = lambda *shape: pl.BlockSpec((gps,) + shape, lambda i: (j0 + i,) + (0,) * len(shape))
    act = lambda *shape: pl.BlockSpec((gps,) + shape, lambda i: (i,) + (0,) * len(shape))
    return pl.pallas_call(
        body,
        grid=(g // gps,),
        in_specs=[
            act(blk, LANES),
            par(nc * nc // 2, LANES),
            par(nc * ch, 4 * p),
            par(nc * ch, 2 * p),
            par(4, 2 * p),
            par(1, nc * ch),
            par(nsr, p),
            par(nsr, p),
        ],
        out_specs=[act(blk, LANES), act(n_streams, p), act(n_streams, p), act(nsr, p), act(nsr, p)],
        out_shape=[
            jax.ShapeDtypeStruct(u.shape, F32),
            jax.ShapeDtypeStruct((g, n_streams, p), F32),
            jax.ShapeDtypeStruct((g, n_streams, p), F32),
            jax.ShapeDtypeStruct((g, nsr, p), F32),
            jax.ShapeDtypeStruct((g, nsr, p), F32),
        ],
        scratch_shapes=[pltpu.VMEM((gps, nc * ch, nc * ch), BF16)],
        compiler_params=_params(),
        name="s5_scan",
    )(u, kmat, wst, cpow, lam, dl, h0r, h0i)


def _s5_out_body(x_ref, z_ref, gpost_ref, wglu_ref, o_ref, *, nc):
    tt, d = x_ref.shape
    g = z_ref.shape[0]
    nq = nc // 2
    low = _lane_is_low((g, nq, LANES))
    blocks = []
    for m in range(tt // LANES):
        e = z_ref[:, (2 * m) * nq:(2 * m + 1) * nq, :]
        o = z_ref[:, (2 * m + 1) * nq:(2 * m + 2) * nq, :]
        a = jnp.where(low, e, pltpu.roll(o, S5_CHUNK, 2))
        b = jnp.where(low, pltpu.roll(e, S5_CHUNK, 2), o)
        blocks.append(jnp.concatenate([a, b], axis=1).reshape(g * nc, LANES))
    zt = jnp.concatenate(blocks, axis=1)
    gate = lax.dot_general(zt.astype(BF16), _w(wglu_ref, 0), _TN, preferred_element_type=F32)
    out = gate[:, :d] * jax.nn.sigmoid(gate[:, d:])
    o_ref[...] = x_ref[...] + _rms(out, gpost_ref[0])


def _s5_out(x, z, npost, wglu, layer, j, nc, tt):
    nt, d = x.shape
    dm = wglu.shape[1]
    g = dm // nc
    nidx = layer * 4 + 1
    rows_per_tile = tt // S5_CHUNK * (nc // 2)
    return pl.pallas_call(
        functools.partial(_s5_out_body, nc=nc),
        grid=(nt // tt,),
        in_specs=[
            pl.BlockSpec((tt, d), lambda i: (i, 0)),
            pl.BlockSpec((g, rows_per_tile, LANES), lambda i: (0, i, 0)),
            pl.BlockSpec((1, 1, d), lambda i: (nidx, 0, 0)),
            pl.BlockSpec((1, dm, 2 * d), lambda i: (j, 0, 0)),
        ],
        out_specs=pl.BlockSpec((tt, d), lambda i: (i, 0)),
        out_shape=jax.ShapeDtypeStruct(x.shape, x.dtype),
        compiler_params=_params(),
        name="s5_out",
    )(x, z, npost, wglu)


def kernel(x_prompt, x_sample, cache_mem_k, cache_mem_v, state_ssm_re, state_ssm_im, state_pool,
           mem_prompt, norm_pre, norm_post, norm_mem, ffn_w_gate, ffn_w_up, ffn_w_down,
           xa_w_q, xa_w_k, xa_w_v, xa_w_o,
           s5_w_in, s5_a_re, s5_a_im, s5_b_re, s5_b_im, s5_c_re, s5_c_im, s5_d, s5_log_dt, s5_w_glu,
           pool_w_in, pool_w_grp, pool_b_grp, pool_scale):
    b, seq, d = x_prompt.shape
    db, ds, _ = x_sample.shape
    depth, nsub, _ = norm_pre.shape
    nm, nh = cache_mem_k.shape[2], cache_mem_k.shape[3]
    na, g, p, nc = s5_b_re.shape
    nb_layers = pool_w_in.shape[0]
    f = ffn_w_gate.shape[-1]
    ch = S5_CHUNK
    assert ds <= ch and ds % SUBLANES == 0 and ds >= POOL_HIST and PAST_LEN >= POOL_HIST
    assert nc == 2 * SUBLANES and g * nc == d and 2 * p == LANES

    n_prompt = b * seq
    nt = n_prompt + db * ch
    tm = min(512, seq)
    assert seq % tm == 0 and tm % LANES == 0 and nt % tm == 0

    w_ffn = (ffn_w_gate.reshape(depth * 2, d, f), ffn_w_up.reshape(depth * 2, d, f),
             ffn_w_down.reshape(depth * 2, f, d))
    n_ffn = depth * 2

    def ffn(xs_, k, rounded, split_out=False):
        layer, which = divmod(k, 2)
        return _ffn(xs_, npre, npost, rounded or w_ffn, 0 if rounded else k,
                    layer * nsub + (0 if which == 0 else nsub - 1), tm, npt, split_out=split_out,
                    w_next=(w_ffn, k + 1) if k + 1 < n_ffn else None)
    wq, wk, wv, wo = xa_w_q, xa_w_k, xa_w_v, xa_w_o
    s5_wt = jnp.swapaxes(s5_w_in, 1, 2).astype(BF16)
    s5_glu = s5_w_glu
    p_win, p_wgrp = pool_w_in, pool_w_grp
    npre = norm_pre.reshape(depth * nsub, 1, d)
    npost = norm_post.reshape(depth * nsub, 1, d)

    assert db * ch == tm
    npt = n_prompt // tm
    xs = (x_prompt.reshape(n_prompt, d),
          jnp.pad(x_sample, ((0, 0), (0, ch - ds), (0, 0))).reshape(db * ch, d))

    mem_k_p, mem_v_p, kb_p, vb_p = _mem_kv(mem_prompt, norm_mem.reshape(depth, 1, d), wk, wv, nh)

    kmat, wst, cpow, lam = _s5_params(s5_a_re, s5_a_im, s5_log_dt, s5_b_re, s5_b_im,
                                      s5_c_re, s5_c_im, ds)
    slot_channel = [_slot_channel(slot, nc) for slot in range(nc)]
    dl = jnp.repeat(s5_d.reshape(na * g, 1, nc)[:, :, slot_channel], ch, axis=2)
    h0r = jnp.swapaxes(state_ssm_re, 1, 2).reshape(na * g, db, p)
    h0i = jnp.swapaxes(state_ssm_im, 1, 2).reshape(na * g, db, p)

    hist_p = jnp.zeros((nb_layers, b, HIST_ROWS, d), F32)
    hist_s = jnp.pad(state_pool, ((0, 0), (0, 0), (HIST_ROWS - POOL_HIST, 0), (0, 0)))
    pb = jnp.reshape(pool_b_grp, (nb_layers, 1, d))
    ps = jnp.reshape(pool_scale, (nb_layers, 1, d))

    ssm_p_re, ssm_p_im, ssm_s_re, ssm_s_im, buf_p, buf_s = [], [], [], [], [], []
    w_rounded = None
    for i in range(depth):
        j = i // 2
        x, w_rounded = ffn(xs if i == 0 else (x,), 2 * i, w_rounded)
        if i % 2 == 0:
            u = _s5_in(x, npre, s5_wt, i, j, nc, tm)
            z, hpr, hpi, hsr, hsi = _s5_scan(
                u, kmat, wst, cpow, lam, dl, h0r, h0i, j, nc, n_prompt // ch, seq // ch, b)
            x = _s5_out(x, z, npost, s5_glu, i, j, nc, tm)
            ssm_p_re.append(jnp.swapaxes(hpr, 0, 1))
            ssm_p_im.append(jnp.swapaxes(hpi, 0, 1))
            ssm_s_re.append(jnp.swapaxes(hsr, 0, 1))
            ssm_s_im.append(jnp.swapaxes(hsi, 0, 1))
        else:
            tp = min(POOL_TILE, seq)
            x, bp = _pool(x, npre, npost, p_win, p_wgrp, pb, ps, hist_p, i, j, 0, seq, tp, 0, tp)
            x, bs = _pool(x, npre, npost, p_win, p_wgrp, pb, ps, hist_s, i, j, n_prompt, ch, ds,
                          PAST_LEN, ch)
            buf_p.append(bp)
            buf_s.append(bs)
        x = _attn(x, npre, npost, wq, wo, kb_p, vb_p, i, nh, seq, min(ATTN_TILE, seq))
        x = _attn_sample(x, npre, npost, wq, wo, cache_mem_k, cache_mem_v, i, n_prompt, ch)
        x, w_rounded = ffn((x,), 2 * i + 1, w_rounded, split_out=(i == depth - 1))

    y_prompt = x[0].reshape(b, seq, d)
    y_sample = x[1].reshape(db, ch, d)[:, :ds, :]
    return (y_prompt, y_sample, mem_k_p, mem_v_p,
            jnp.stack(ssm_p_re), jnp.stack(ssm_p_im), jnp.stack(buf_p),
            jnp.stack(ssm_s_re), jnp.stack(ssm_s_im), jnp.stack(buf_s))
```

```python
import functools

import jax
import jax.numpy as jnp
from jax import lax
from jax.experimental import pallas as pl
from jax.experimental.pallas import tpu as pltpu

F32 = jnp.float32
BF16 = jnp.bfloat16

RMS_EPS = 1e-6
PAST_LEN = 1024
POOL_WINDOWS = (2, 4, 8, 16)
POOL_HIST = max(POOL_WINDOWS) - 1
HIST_ROWS = 16

LANES = 128
SUBLANES = 8
S5_CHUNK = LANES // 2
S5_GROUPS_PER_STEP = 2
S5_PARAM_GROUPS_PER_STEP = 4
FFN_ROW_CHUNK = 256
ATTN_TILE = 1024
POOL_TILE = 1024
VMEM_LIMIT = 56 * 1024 * 1024

_NT = (((1,), (1,)), ((), ()))
_TN = (((0,), (0,)), ((), ()))


def _params(n_axes=1, vmem=VMEM_LIMIT):
    return pltpu.CompilerParams(
        dimension_semantics=("arbitrary",) * n_axes, vmem_limit_bytes=vmem)


def _rms(x, g):
    ms = jnp.mean(x * x, axis=-1, keepdims=True)
    return x * lax.rsqrt(ms + RMS_EPS) * g


def _dot(a, b):
    return jnp.dot(a, b, preferred_element_type=F32)


def _w(ref, *idx):
    return ref[idx].astype(BF16)


def _dot_nt(a, b, precision=None):
    return lax.dot_general(a, b, _NT, preferred_element_type=F32, precision=precision)


def _ffn_body(*refs, n_in, n_out, n_next, n_prompt_tiles, row_chunk):
    x_refs, (gpre_ref, gpost_ref, wg_ref, wu_ref, wd_ref) = refs[:n_in], refs[n_in:n_in + 5]
    next_refs = refs[n_in + 5:n_in + 5 + n_next]
    o_refs = refs[n_in + 5 + n_next:n_in + 5 + n_next + n_out]
    rounded_refs = refs[n_in + 5 + n_next + n_out:]
    i = pl.program_id(0)
    tm = x_refs[0].shape[0]
    on_prompt = i < n_prompt_tiles
    for r in range(tm // row_chunk):
        rows = slice(r * row_chunk, (r + 1) * row_chunk)
        x = x_refs[0][rows, :]
        if n_in == 2:
            x = jnp.where(on_prompt, x, x_refs[1][rows, :])
        hn = _rms(x, gpre_ref[0]).astype(BF16)
        g = _dot(hn, _w(wg_ref, 0))
        u = _dot(hn, _w(wu_ref, 0))
        a = (g * jax.nn.sigmoid(g) * u).astype(BF16)
        o_refs[-1][rows, :] = x + 0.5 * _rms(_dot(a, _w(wd_ref, 0)), gpost_ref[0])
    if n_out == 2:
        @pl.when(on_prompt)
        def _():
            o_refs[0][...] = o_refs[1][...]
    _round_slabs(next_refs, rounded_refs)


def _round_slabs(src_refs, dst_refs):
    for src, dst in zip(src_refs, dst_refs):
        dst[...] = src[...].astype(BF16)


def _slab_rows(rows, n_steps):
    need = -(-rows // n_steps)
    return min(r for r in range(16, rows + 1, 16) if rows % r == 0 and r >= need)


def _rounding_slabs(w_next, n_steps):
    args, in_specs, out_specs, out_shape = (), [], [], []
    if w_next is not None:
        stacks, nxt = w_next
        for a in stacks:
            rows, cols = a.shape[1], a.shape[2]
            slab = _slab_rows(rows, n_steps)
            last = rows // slab - 1
            args += (a,)
            in_specs.append(pl.BlockSpec(
                (1, slab, cols), lambda i, last=last: (nxt, jnp.minimum(i, last), 0)))
            out_specs.append(pl.BlockSpec(
                (1, slab, cols), lambda i, last=last: (0, jnp.minimum(i, last), 0)))
            out_shape.append(jax.ShapeDtypeStruct((1, rows, cols), BF16))
    return args, in_specs, out_specs, out_shape


def _ffn(xs, npre, npost, w, widx, nidx, tm, n_prompt_tiles, split_out=False, w_next=None):
    d = xs[0].shape[1]
    f = w[0].shape[-1]
    last_prompt = n_prompt_tiles - 1
    joint = pl.BlockSpec((tm, d), lambda i: (i, 0))
    prompt = pl.BlockSpec((tm, d), lambda i: (jnp.minimum(i, last_prompt), 0))
    sample = pl.BlockSpec((tm, d), lambda i: (0, 0))
    once = dict(pipeline_mode=pl.Buffered(1))
    if len(xs) == 2:
        assert xs[0].shape[0] == n_prompt_tiles * tm and xs[1].shape[0] == tm
    n_steps = n_prompt_tiles + 1
    nt = n_steps * tm
    if split_out:
        out_specs = [prompt, sample]
        out_shape = [jax.ShapeDtypeStruct((n_prompt_tiles * tm, d), F32),
                     jax.ShapeDtypeStruct((tm, d), F32)]
    else:
        out_specs = [joint]
        out_shape = [jax.ShapeDtypeStruct((nt, d), F32)]
    next_args, next_specs, next_out, next_shape = _rounding_slabs(w_next, n_steps)
    out_specs, out_shape = out_specs + next_out, out_shape + next_shape
    body = functools.partial(_ffn_body, n_in=len(xs), n_out=2 if split_out else 1,
                             n_next=len(next_args), n_prompt_tiles=n_prompt_tiles,
                             row_chunk=FFN_ROW_CHUNK)
    outs = pl.pallas_call(
        body,
        grid=(n_steps,),
        in_specs=([joint] if len(xs) == 1 else [prompt, sample]) + [
            pl.BlockSpec((1, 1, d), lambda i: (nidx, 0, 0)),
            pl.BlockSpec((1, 1, d), lambda i: (nidx, 0, 0)),
            pl.BlockSpec((1, d, f), lambda i: (widx, 0, 0), **once),
            pl.BlockSpec((1, d, f), lambda i: (widx, 0, 0), **once),
            pl.BlockSpec((1, f, d), lambda i: (widx, 0, 0), **once),
        ] + next_specs,
        out_specs=out_specs,
        out_shape=out_shape,
        compiler_params=_params(),
        name="ffn",
    )(*xs, npre, npost, *w, *next_args)
    n_y = 2 if split_out else 1
    y = tuple(outs[:n_y]) if split_out else outs[0]
    return y, (tuple(outs[n_y:]) if w_next is not None else None)


def _kv_body(m_ref, g_ref, wk_ref, wv_ref, k_ref, v_ref, kb_ref, vb_ref):
    nh, hd = k_ref.shape[3], k_ref.shape[4]
    m = _rms(m_ref[0], g_ref[0]).astype(BF16)
    for w_ref, out_ref, bf_ref in ((wk_ref, k_ref, kb_ref), (wv_ref, v_ref, vb_ref)):
        y = _dot(m, _w(w_ref, 0))
        bf_ref[0, 0] = y.astype(BF16)
        for h in range(nh):
            out_ref[0, 0, :, h, :] = y[:, h * hd:(h + 1) * hd]


def _mem_kv(mem, nmem, wk, wv, nh):
    b, nm, d = mem.shape
    depth = wk.shape[0]
    hd = d // nh
    out = jax.ShapeDtypeStruct((depth, b, nm, nh, hd), F32)
    out_b = jax.ShapeDtypeStruct((depth, b, nm, d), BF16)
    return pl.pallas_call(
        _kv_body,
        grid=(depth, b),
        in_specs=[
            pl.BlockSpec((1, nm, d), lambda l, j: (j, 0, 0)),
            pl.BlockSpec((1, 1, d), lambda l, j: (l, 0, 0)),
            pl.BlockSpec((1, d, d), lambda l, j: (l, 0, 0)),
            pl.BlockSpec((1, d, d), lambda l, j: (l, 0, 0)),
        ],
        out_specs=[pl.BlockSpec((1, 1, nm, nh, hd), lambda l, j: (l, j, 0, 0, 0))] * 2
        + [pl.BlockSpec((1, 1, nm, d), lambda l, j: (l, j, 0, 0))] * 2,
        out_shape=[out, out, out_b, out_b],
        compiler_params=_params(2),
        name="mem_kv",
    )(mem, nmem, wk, wv)


def _attn_rows(x, gpre, gpost, wq, wo, kv, nh):
    hd = x.shape[-1] // nh
    hn = _rms(x, gpre).astype(BF16)
    q = (_dot(hn, wq) * (hd ** -0.5)).astype(BF16)
    heads = []
    for h, (kh, vh) in enumerate(kv):
        s = _dot_nt(q[:, h * hd:(h + 1) * hd], kh)
        e = jnp.exp(s - jnp.max(s, axis=-1, keepdims=True))
        p = e / jnp.sum(e, axis=-1, keepdims=True)
        heads.append(_dot(p.astype(BF16), vh))
    o = jnp.concatenate(heads, axis=-1).astype(BF16)
    return x + _rms(_dot(o, wo), gpost)


def _attn_body(x_ref, gpre_ref, gpost_ref, wq_ref, wo_ref, k_ref, v_ref, o_ref, *, nh):
    hd = x_ref.shape[1] // nh
    kv = [(k_ref[0, 0, :, h * hd:(h + 1) * hd], v_ref[0, 0, :, h * hd:(h + 1) * hd])
          for h in range(nh)]
    o_ref[...] = _attn_rows(x_ref[...], gpre_ref[0], gpost_ref[0], _w(wq_ref, 0), _w(wo_ref, 0),
                            kv, nh)


def _attn(x, npre, npost, wq, wo, k, v, layer, nh, rows_per_batch, ta):
    nt, d = x.shape
    depth, nb, nm, _ = k.shape
    nsub = npre.shape[0] // depth
    nidx = layer * nsub + 2
    tpb = rows_per_batch // ta
    kv_spec = pl.BlockSpec((1, 1, nm, d), lambda i: (layer, i // tpb, 0, 0))
    return pl.pallas_call(
        functools.partial(_attn_body, nh=nh),
        grid=(nb * tpb,),
        in_specs=[
            pl.BlockSpec((ta, d), lambda i: (i, 0)),
            pl.BlockSpec((1, 1, d), lambda i: (nidx, 0, 0)),
            pl.BlockSpec((1, 1, d), lambda i: (nidx, 0, 0)),
            pl.BlockSpec((1, d, d), lambda i: (layer, 0, 0), pipeline_mode=pl.Buffered(1)),
            pl.BlockSpec((1, d, d), lambda i: (layer, 0, 0), pipeline_mode=pl.Buffered(1)),
            kv_spec,
            kv_spec,
        ],
        out_specs=pl.BlockSpec((ta, d), lambda i: (i, 0)),
        out_shape=jax.ShapeDtypeStruct(x.shape, x.dtype),
        input_output_aliases={0: 0},
        compiler_params=_params(),
        name="xattn",
    )(x, npre, npost, wq, wo, k, v)


def _head_copies(k_hbm, v_hbm, kbuf, vbuf, sem, layer, stream):
    copies = []
    for h in range(kbuf.shape[1]):
        copies.append(pltpu.make_async_copy(
            k_hbm.at[layer, stream, :, h, :], kbuf.at[stream, h], sem.at[stream, 0]))
        copies.append(pltpu.make_async_copy(
            v_hbm.at[layer, stream, :, h, :], vbuf.at[stream, h], sem.at[stream, 1]))
    return copies


def _attn_sample_body(x_ref, gpre_ref, gpost_ref, wq_ref, wo_ref, k_hbm, v_hbm, o_ref,
                      kbuf, vbuf, sem, *, layer, rows):
    nb, nh = kbuf.shape[0], kbuf.shape[1]
    for s in range(nb):
        for c in _head_copies(k_hbm, v_hbm, kbuf, vbuf, sem, layer, s):
            c.start()
    x = x_ref[...]
    hd = x.shape[-1] // nh
    hn = _rms(x, gpre_ref[0]).astype(BF16)
    q = (_dot(hn, _w(wq_ref, 0)) * (hd ** -0.5)).astype(BF16)
    outs = []
    for s in range(nb):
        for c in _head_copies(k_hbm, v_hbm, kbuf, vbuf, sem, layer, s):
            c.wait()
        heads = []
        for h in range(nh):
            sc = _dot_nt(q[s * rows:(s + 1) * rows, h * hd:(h + 1) * hd], kbuf[s, h].astype(BF16))
            e = jnp.exp(sc - jnp.max(sc, axis=-1, keepdims=True))
            p = e / jnp.sum(e, axis=-1, keepdims=True)
            heads.append(_dot(p.astype(BF16), vbuf[s, h].astype(BF16)))
        outs.append(jnp.concatenate(heads, axis=-1))
    o = jnp.concatenate(outs, axis=0).astype(BF16)
    o_ref[...] = x + _rms(_dot(o, _w(wo_ref, 0)), gpost_ref[0])


def _attn_sample(x, npre, npost, wq, wo, k, v, layer, row0, rows):
    nt, d = x.shape
    depth, nb, nm, nh, hd = k.shape
    nsub = npre.shape[0] // depth
    nidx = layer * nsub + 2
    tile = nb * rows
    assert row0 % tile == 0
    t0 = row0 // tile
    once = dict(pipeline_mode=pl.Buffered(1))
    return pl.pallas_call(
        functools.partial(_attn_sample_body, layer=layer, rows=rows),
        grid=(1,),
        in_specs=[
            pl.BlockSpec((tile, d), lambda i: (t0, 0)),
            pl.BlockSpec((1, 1, d), lambda i: (nidx, 0, 0)),
            pl.BlockSpec((1, 1, d), lambda i: (nidx, 0, 0)),
            pl.BlockSpec((1, d, d), lambda i: (layer, 0, 0), **once),
            pl.BlockSpec((1, d, d), lambda i: (layer, 0, 0), **once),
            pl.BlockSpec(memory_space=pl.ANY),
            pl.BlockSpec(memory_space=pl.ANY),
        ],
        out_specs=pl.BlockSpec((tile, d), lambda i: (t0, 0)),
        out_shape=jax.ShapeDtypeStruct(x.shape, x.dtype),
        scratch_shapes=[pltpu.VMEM((nb, nh, nm, hd), F32), pltpu.VMEM((nb, nh, nm, hd), F32),
                        pltpu.SemaphoreType.DMA((nb, 2))],
        input_output_aliases={0: 0},
        compiler_params=_params(),
        name="xattn_sample",
    )(x, npre, npost, wq, wo, k, v)


def _pool_body(x_ref, gpre_ref, gpost_ref, win_ref, wgrp_ref, bgrp_ref, scale_ref, hist_ref,
               o_ref, buf_ref, ext_ref, *, tiles_per_batch, pos0, valid_rows):
    i = pl.program_id(0)
    tt, d = x_ref.shape
    ng = len(POOL_WINDOWS)
    gw = d // ng
    first = (i % tiles_per_batch) == 0

    x = x_ref[...]
    hn = _rms(x, gpre_ref[0]).astype(BF16)
    u = _dot(hn, _w(win_ref, 0))

    @pl.when(first)
    def _():
        ext_ref[0:HIST_ROWS, :] = hist_ref[0, 0]

    @pl.when(jnp.logical_not(first))
    def _():
        ext_ref[0:HIST_ROWS, :] = ext_ref[tt:tt + HIST_ROWS, :]

    ext_ref[HIST_ROWS:HIST_ROWS + tt, :] = u

    tile_pos = pos0 + (i % tiles_per_batch) * tt
    pos = tile_pos + lax.broadcasted_iota(jnp.int32, (tt, 1), 0)
    level = ext_ref[...]
    parts = []
    for g, w in enumerate(POOL_WINDOWS):
        level = level + pltpu.roll(level, w // 2, 0)
        cnt = jnp.minimum(pos + 1, w).astype(F32)
        parts.append(level[HIST_ROWS:HIST_ROWS + tt, 0:gw] * (1.0 / cnt))
        if g + 1 < ng:
            level = level[:, gw:]
    pooled = (jnp.concatenate(parts, axis=-1) - u).astype(BF16)

    mixed = []
    for g in range(ng):
        sl = slice(g * gw, (g + 1) * gw)
        mixed.append(_dot(pooled[:, sl], _w(wgrp_ref, 0, g)) + bgrp_ref[0, :, sl])
    out = jnp.concatenate(mixed, axis=-1) * scale_ref[0]
    o_ref[...] = x + _rms(out, gpost_ref[0])

    @pl.when((i % tiles_per_batch) == tiles_per_batch - 1)
    def _():
        buf_ref[0] = u[valid_rows - POOL_HIST:valid_rows, :]


def _pool(x, npre, npost, win, wgrp, bgrp, scale, hist, layer, j, row0, rows_per_batch,
          valid_rows_last, pos0, tt):
    nt, d = x.shape
    nb = hist.shape[1]
    nidx = layer * 4 + 1
    ng = len(POOL_WINDOWS)
    gw = d // ng
    t0 = row0 // tt
    tpb = rows_per_batch // tt
    body = functools.partial(_pool_body, tiles_per_batch=tpb, pos0=pos0, valid_rows=valid_rows_last)
    return pl.pallas_call(
        body,
        grid=(nb * tpb,),
        in_specs=[
            pl.BlockSpec((tt, d), lambda i: (t0 + i, 0)),
            pl.BlockSpec((1, 1, d), lambda i: (nidx, 0, 0)),
            pl.BlockSpec((1, 1, d), lambda i: (nidx, 0, 0)),
            pl.BlockSpec((1, d, d), lambda i: (j, 0, 0)),
            pl.BlockSpec((1, ng, gw, gw), lambda i: (j, 0, 0, 0)),
            pl.BlockSpec((1, 1, d), lambda i: (j, 0, 0)),
            pl.BlockSpec((1, 1, d), lambda i: (j, 0, 0)),
            pl.BlockSpec((1, 1, HIST_ROWS, d), lambda i: (j, i // tpb, 0, 0)),
        ],
        out_specs=[
            pl.BlockSpec((tt, d), lambda i: (t0 + i, 0)),
            pl.BlockSpec((1, POOL_HIST, d), lambda i: (i // tpb, 0, 0)),
        ],
        out_shape=[
            jax.ShapeDtypeStruct(x.shape, x.dtype),
            jax.ShapeDtypeStruct((nb, POOL_HIST, d), F32),
        ],
        scratch_shapes=[pltpu.VMEM((HIST_ROWS + tt, d), F32)],
        input_output_aliases={0: 0},
        compiler_params=_params(),
        name="pool_mixer",
    )(x, npre, npost, win, wgrp, bgrp, scale, hist)


def _cmul(ar, ai, br, bi):
    return ar * br - ai * bi, ar * bi + ai * br


def _cpow(zr, zi, n):
    rr, ri = None, None
    while n:
        if n & 1:
            rr, ri = (zr, zi) if rr is None else _cmul(rr, ri, zr, zi)
        n >>= 1
        if n:
            zr, zi = _cmul(zr, zi, zr, zi)
    return rr, ri


def _slot_channel(slot, nc):
    return slot // 2 + (nc // 2) * (slot % 2)


def _s5_param_group(gi, are_ref, aim_ref, ldt_ref, btr_ref, bti_ref, cr_ref, ci_ref,
                    kmat_ref, wst_ref, cpow_ref, lam_ref, *, sample_len):
    ch = S5_CHUNK
    nc, p = btr_ref.shape[1], btr_ref.shape[2]
    nq = nc // 2
    dt = jnp.exp(ldt_ref[gi])
    lre, lim = are_ref[gi], aim_ref[gi]
    zr, zi = lre * dt, lim * dt
    mag = jnp.exp(zr)
    lbr, lbi = mag * jnp.cos(zi), mag * jnp.sin(zi)
    den = lre * lre + lim * lim
    qr = ((lbr - 1.0) * lre + lbi * lim) / den
    qi = (lbi * lre - (lbr - 1.0) * lim) / den
    bbr, bbi = _cmul(qr, qi, btr_ref[gi], bti_ref[gi])

    n8 = lax.broadcasted_iota(jnp.int32, (SUBLANES, 1), 0)

    def base(expo):
        m = jnp.exp(expo * zr)
        a = expo * zi
        return m * jnp.cos(a), m * jnp.sin(a)

    asc_r, asc_i = base(n8.astype(F32))
    rev_r, rev_i = base((SUBLANES - 1 - n8).astype(F32))
    rows = SUBLANES
    while rows < ch:
        sr, si = _cpow(lbr, lbi, rows)
        ur, ui = _cmul(asc_r, asc_i, sr, si)
        asc_r, asc_i = jnp.concatenate([asc_r, ur], axis=0), jnp.concatenate([asc_i, ui], axis=0)
        ur, ui = _cmul(rev_r, rev_i, sr, si)
        rev_r, rev_i = jnp.concatenate([ur, rev_r], axis=0), jnp.concatenate([ui, rev_i], axis=0)
        rows *= 2
    p1r, p1i = _cmul(asc_r, asc_i, lbr, lbi)
    p1_ri = jnp.concatenate([p1r, p1i], axis=1)
    p1_ir = jnp.concatenate([p1i, p1r], axis=1)

    dead = jnp.zeros((ch - sample_len, p), F32)
    rev2_r = jnp.concatenate(
        [rev_r, jnp.concatenate([rev_r[ch - sample_len:], dead], axis=0)], axis=1)
    rev2_i = jnp.concatenate(
        [rev_i, jnp.concatenate([rev_i[ch - sample_len:], dead], axis=0)], axis=1)
    bb2_r = jnp.concatenate([bbr, bbr], axis=1)
    bb2_i = jnp.concatenate([bbi, bbi], axis=1)
    cr, ci = cr_ref[gi], ci_ref[gi]
    c_pm = jnp.concatenate([cr, -cr], axis=1)
    c_mm = jnp.concatenate([-ci, -ci], axis=1)
    w_r = [[], []]
    w_i = [[], []]
    for slot in range(nc):
        c = _slot_channel(slot, nc)
        wr, wi = _cmul(rev2_r, rev2_i, bb2_r[c:c + 1, :], bb2_i[c:c + 1, :])
        wst_ref[gi, slot * ch:(slot + 1) * ch, :] = jnp.concatenate([wr, wi], axis=1).astype(BF16)
        cpow_ref[gi, slot * ch:(slot + 1) * ch, :] = (
            p1_ri * c_pm[c:c + 1, :] + p1_ir * c_mm[c:c + 1, :]).astype(BF16)
        for h in range(2):
            r_, i_ = _cmul(cr[h * nq:(h + 1) * nq, :], ci[h * nq:(h + 1) * nq, :],
                           bbr[c:c + 1, :], bbi[c:c + 1, :])
            w_r[h].append(r_)
            w_i[h].append(i_)

    hi = lax.Precision.HIGHEST
    halves = [_dot_nt(jnp.concatenate(w_r[h], axis=0), asc_r, hi)
              - _dot_nt(jnp.concatenate(w_i[h], axis=0), asc_i, hi) for h in range(2)]
    kmat_ref[gi] = jnp.concatenate(halves, axis=1)

    llr, lli = _cpow(lbr, lbi, ch)
    lsr, lsi = _cpow(lbr, lbi, sample_len)
    lam = jnp.concatenate([llr, lli, lsr, lsi], axis=0)
    lam_ref[gi] = jnp.concatenate([lam, lam], axis=1)


def _s5_param_body(*refs, groups, sample_len, n_next):
    ins, outs = refs[:7], refs[7 + n_next:11 + n_next]
    for gi in range(groups):
        _s5_param_group(gi, *ins, *outs, sample_len=sample_len)
    _round_slabs(refs[7:7 + n_next], refs[11 + n_next:])


def _s5_params(a_re, a_im, log_dt, b_re, b_im, c_re, c_im, sample_len, w_next=None):
    na, g, p, nc = b_re.shape
    n = na * g
    ch = S5_CHUNK
    gps = S5_PARAM_GROUPS_PER_STEP
    assert n % gps == 0
    bt = lambda a: jnp.swapaxes(a, 2, 3).reshape(n, nc, p)
    spec = lambda *shape: pl.BlockSpec((gps,) + shape, lambda i: (i,) + (0,) * len(shape))
    next_args, next_in, next_out, next_shape = _rounding_slabs(w_next, n // gps)
    outs = pl.pallas_call(
        functools.partial(_s5_param_body, groups=gps, sample_len=sample_len,
                          n_next=len(next_args)),
        grid=(n // gps,),
        in_specs=[spec(1, p), spec(1, p), spec(1, 1),
                  spec(nc, p), spec(nc, p), spec(nc, p), spec(nc, p)] + next_in,
        out_specs=[spec(nc * nc // 2, LANES), spec(nc * ch, 4 * p), spec(nc * ch, 2 * p),
                   spec(4, 2 * p)] + next_out,
        out_shape=[
            jax.ShapeDtypeStruct((n, nc * nc // 2, LANES), F32),
            jax.ShapeDtypeStruct((n, nc * ch, 4 * p), BF16),
            jax.ShapeDtypeStruct((n, nc * ch, 2 * p), BF16),
            jax.ShapeDtypeStruct((n, 4, 2 * p), F32),
        ] + next_shape,
        compiler_params=_params(),
        name="s5_params",
    )(a_re.reshape(n, 1, p), a_im.reshape(n, 1, p), log_dt.reshape(n, 1, 1),
      bt(b_re), bt(b_im), c_re.reshape(n, nc, p), c_im.reshape(n, nc, p), *next_args)
    return tuple(outs[:4]), (tuple(outs[4:]) if w_next is not None else None)


def _lane_is_low(shape):
    return lax.broadcasted_iota(jnp.int32, shape, len(shape) - 1) < S5_CHUNK


def _s5_in_body(x_ref, gpre_ref, wt_ref, u_ref, *, nc):
    tt = x_ref.shape[0]
    nq = nc // 2
    hn = _rms(x_ref[...], gpre_ref[0]).astype(BF16)
    ut = _dot_nt(wt_ref[0], hn)
    ut = ut.reshape(ut.shape[0] // nc, nc, tt)
    low = _lane_is_low((ut.shape[0], nq, LANES))
    for m in range(tt // LANES):
        a = ut[:, 0:nq, m * LANES:(m + 1) * LANES]
        b = ut[:, nq:nc, m * LANES:(m + 1) * LANES]
        u_ref[:, (2 * m) * nq:(2 * m + 1) * nq, :] = jnp.where(low, a, pltpu.roll(b, S5_CHUNK, 2))
        u_ref[:, (2 * m + 1) * nq:(2 * m + 2) * nq, :] = jnp.where(low, pltpu.roll(a, S5_CHUNK, 2), b)


def _s5_in(x, npre, wt, layer, j, nc, tt):
    nt, d = x.shape
    dm = wt.shape[1]
    g = dm // nc
    nidx = layer * 4 + 1
    rows_per_tile = tt // S5_CHUNK * (nc // 2)
    return pl.pallas_call(
        functools.partial(_s5_in_body, nc=nc),
        grid=(nt // tt,),
        in_specs=[
            pl.BlockSpec((tt, d), lambda i: (i, 0)),
            pl.BlockSpec((1, 1, d), lambda i: (nidx, 0, 0)),
            pl.BlockSpec((1, dm, d), lambda i: (j, 0, 0)),
        ],
        out_specs=pl.BlockSpec((g, rows_per_tile, LANES), lambda i: (0, i, 0)),
        out_shape=jax.ShapeDtypeStruct((g, nt // S5_CHUNK * (nc // 2), LANES), F32),
        compiler_params=_params(),
        name="s5_in",
    )(x, npre, wt)


def _s5_group(gi, u_ref, kmat_ref, wst_ref, cpow_ref, lam_ref, dl_ref, h0r_ref, h0i_ref,
              z_ref, hpr_ref, hpi_ref, hsr_ref, hsi_ref, t_ref, *, nc, n_prompt_rows,
              rows_per_stream, n_streams):
    ch = S5_CHUNK
    nq = nc // 2
    nrows = u_ref.shape[1] // nq
    p = h0r_ref.shape[2]
    npr = n_prompt_rows
    rps = rows_per_stream

    row_i = lax.broadcasted_iota(jnp.int32, (ch, LANES), 0)
    col_i = lax.broadcasted_iota(jnp.int32, (ch, LANES), 1)
    causal = (col_i % ch) >= row_i
    for slot in range(nc):
        kr = kmat_ref[gi, slot * nq:(slot + 1) * nq, :]
        for q in range(nq):
            blk = jnp.broadcast_to(kr[q:q + 1, :], (ch, LANES))
            blk = pltpu.roll(blk, 0, 1, stride=1, stride_axis=0)
            t_ref[gi, slot * ch:(slot + 1) * ch, q * LANES:(q + 1) * LANES] = (
                jnp.where(causal, blk, 0.0).astype(BF16))

    lhs = jnp.concatenate(
        [u_ref[gi, pl.ds(q, nrows, stride=nq), :] for q in range(nq)], axis=1)
    lhs_b = lhs.astype(BF16)

    y = _dot(lhs_b, t_ref[gi])
    hl = _dot(lhs_b, wst_ref[gi])
    hl_r, hl_i = hl[:, 0:2 * p], hl[:, 2 * p:4 * p]

    lam = lam_ref[gi]
    hr, hi = hl_r[0:npr], hl_i[0:npr]
    k_in_stream = lax.broadcasted_iota(jnp.int32, (npr, 1), 0) % rps
    ar, ai = lam[0:1, :], lam[1:2, :]
    dist = 1
    while dist < rps:
        sr = jnp.where(k_in_stream >= dist, pltpu.roll(hr, dist, 0), 0.0)
        si = jnp.where(k_in_stream >= dist, pltpu.roll(hi, dist, 0), 0.0)
        mr, mi = _cmul(ar, ai, sr, si)
        hr, hi = hr + mr, hi + mi
        ar, ai = _cmul(ar, ai, ar, ai)
        dist *= 2
    for b in range(n_streams):
        last = (b + 1) * rps - 1
        hpr_ref[gi, b:b + 1, :] = hr[last:last + 1, 0:p]
        hpi_ref[gi, b:b + 1, :] = hi[last:last + 1, 0:p]
    ser = jnp.where(k_in_stream >= 1, pltpu.roll(hr, 1, 0), 0.0)[:, 0:p]
    sei = jnp.where(k_in_stream >= 1, pltpu.roll(hi, 1, 0), 0.0)[:, 0:p]

    h0r, h0i = h0r_ref[gi], h0i_ref[gi]
    fr, fi = _cmul(lam[2:3, 0:p], lam[3:4, 0:p], h0r, h0i)
    hsr_ref[gi] = fr + hl_r[npr:, p:2 * p]
    hsi_ref[gi] = fi + hl_i[npr:, p:2 * p]

    s_all = jnp.concatenate([jnp.concatenate([ser, sei], axis=1),
                             jnp.concatenate([h0r, h0i], axis=1)], axis=0).astype(BF16)
    y = y + _dot_nt(s_all, cpow_ref[gi])

    z = jax.nn.gelu(y + dl_ref[gi] * lhs)
    for q in range(nq):
        z_ref[gi, pl.ds(q, nrows, stride=nq), :] = z[:, q * LANES:(q + 1) * LANES]


def _s5_scan_body(*refs, groups, **static):
    for gi in range(groups):
        _s5_group(gi, *refs, **static)


def _s5_scan(u, kmat, wst, cpow, lam, dl, h0r, h0i, j, nc, n_prompt_rows, rows_per_stream,
             n_streams):
    ch = S5_CHUNK
    g, blk, _ = u.shape
    gps = S5_GROUPS_PER_STEP
    assert g % gps == 0
    nrows = blk // (nc // 2)
    p = h0r.shape[2]
    nsr = nrows - n_prompt_rows
    body = functools.partial(_s5_scan_body, groups=gps, nc=nc, n_prompt_rows=n_prompt_rows,
                             rows_per_stream=rows_per_stream, n_streams=n_streams)
    j0 = j * (g // gps)
    par = lambda *shape: pl.BlockSpec((gps,) + shape, lambda i: (j0 + i,) + (0,) * len(shape))
    act = lambda *shape: pl.BlockSpec((gps,) + shape, lambda i: (i,) + (0,) * len(shape))
    return pl.pallas_call(
        body,
        grid=(g // gps,),
        in_specs=[
            act(blk, LANES),
            par(nc * nc // 2, LANES),
            par(nc * ch, 4 * p),
            par(nc * ch, 2 * p),
            par(4, 2 * p),
            par(1, nc * ch),
            par(nsr, p),
            par(nsr, p),
        ],
        out_specs=[act(blk, LANES), act(n_streams, p), act(n_streams, p), act(nsr, p), act(nsr, p)],
        out_shape=[
            jax.ShapeDtypeStruct(u.shape, F32),
            jax.ShapeDtypeStruct((g, n_streams, p), F32),
            jax.ShapeDtypeStruct((g, n_streams, p), F32),
            jax.ShapeDtypeStruct((g, nsr, p), F32),
            jax.ShapeDtypeStruct((g, nsr, p), F32),
        ],
        scratch_shapes=[pltpu.VMEM((gps, nc * ch, nc * ch), BF16)],
        compiler_params=_params(),
        name="s5_scan",
    )(u, kmat, wst, cpow, lam, dl, h0r, h0i)


def _s5_out_body(x_ref, z_ref, gpost_ref, wglu_ref, o_ref, *, nc):
    tt, d = x_ref.shape
    g = z_ref.shape[0]
    nq = nc // 2
    low = _lane_is_low((g, nq, LANES))
    blocks = []
    for m in range(tt // LANES):
        e = z_ref[:, (2 * m) * nq:(2 * m + 1) * nq, :]
        o = z_ref[:, (2 * m + 1) * nq:(2 * m + 2) * nq, :]
        a = jnp.where(low, e, pltpu.roll(o, S5_CHUNK, 2))
        b = jnp.where(low, pltpu.roll(e, S5_CHUNK, 2), o)
        blocks.append(jnp.concatenate([a, b], axis=1).reshape(g * nc, LANES))
    zt = jnp.concatenate(blocks, axis=1)
    gate = lax.dot_general(zt.astype(BF16), _w(wglu_ref, 0), _TN, preferred_element_type=F32)
    out = gate[:, :d] * jax.nn.sigmoid(gate[:, d:])
    o_ref[...] = x_ref[...] + _rms(out, gpost_ref[0])


def _s5_out(x, z, npost, wglu, layer, j, nc, tt):
    nt, d = x.shape
    dm = wglu.shape[1]
    g = dm // nc
    nidx = layer * 4 + 1
    rows_per_tile = tt // S5_CHUNK * (nc // 2)
    return pl.pallas_call(
        functools.partial(_s5_out_body, nc=nc),
        grid=(nt // tt,),
        in_specs=[
            pl.BlockSpec((tt, d), lambda i: (i, 0)),
            pl.BlockSpec((g, rows_per_tile, LANES), lambda i: (0, i, 0)),
            pl.BlockSpec((1, 1, d), lambda i: (nidx, 0, 0)),
            pl.BlockSpec((1, dm, 2 * d), lambda i: (j, 0, 0)),
        ],
        out_specs=pl.BlockSpec((tt, d), lambda i: (i, 0)),
        out_shape=jax.ShapeDtypeStruct(x.shape, x.dtype),
        compiler_params=_params(),
        name="s5_out",
    )(x, z, npost, wglu)


def kernel(x_prompt, x_sample, cache_mem_k, cache_mem_v, state_ssm_re, state_ssm_im, state_pool,
           mem_prompt, norm_pre, norm_post, norm_mem, ffn_w_gate, ffn_w_up, ffn_w_down,
           xa_w_q, xa_w_k, xa_w_v, xa_w_o,
           s5_w_in, s5_a_re, s5_a_im, s5_b_re, s5_b_im, s5_c_re, s5_c_im, s5_d, s5_log_dt, s5_w_glu,
           pool_w_in, pool_w_grp, pool_b_grp, pool_scale):
    b, seq, d = x_prompt.shape
    db, ds, _ = x_sample.shape
    depth, nsub, _ = norm_pre.shape
    nm, nh = cache_mem_k.shape[2], cache_mem_k.shape[3]
    na, g, p, nc = s5_b_re.shape
    nb_layers = pool_w_in.shape[0]
    f = ffn_w_gate.shape[-1]
    ch = S5_CHUNK
    assert ds <= ch and ds % SUBLANES == 0 and ds >= POOL_HIST and PAST_LEN >= POOL_HIST
    assert nc == 2 * SUBLANES and g * nc == d and 2 * p == LANES

    n_prompt = b * seq
    nt = n_prompt + db * ch
    tm = min(512, seq)
    assert seq % tm == 0 and tm % LANES == 0 and nt % tm == 0

    w_ffn = (ffn_w_gate.reshape(depth * 2, d, f), ffn_w_up.reshape(depth * 2, d, f),
             ffn_w_down.reshape(depth * 2, f, d))
    n_ffn = depth * 2

    def ffn(xs_, k, rounded, split_out=False):
        layer, which = divmod(k, 2)
        return _ffn(xs_, npre, npost, rounded or w_ffn, 0 if rounded else k,
                    layer * nsub + (0 if which == 0 else nsub - 1), tm, npt, split_out=split_out,
                    w_next=(w_ffn, k + 1) if k + 1 < n_ffn else None)
    wq, wk, wv, wo = xa_w_q, xa_w_k, xa_w_v, xa_w_o
    s5_wt = jnp.swapaxes(s5_w_in, 1, 2).astype(BF16)
    s5_glu = s5_w_glu
    p_win, p_wgrp = pool_w_in, pool_w_grp
    npre = norm_pre.reshape(depth * nsub, 1, d)
    npost = norm_post.reshape(depth * nsub, 1, d)

    assert db * ch == tm
    npt = n_prompt // tm
    xs = (x_prompt.reshape(n_prompt, d),
          jnp.pad(x_sample, ((0, 0), (0, ch - ds), (0, 0))).reshape(db * ch, d))

    mem_k_p, mem_v_p, kb_p, vb_p = _mem_kv(mem_prompt, norm_mem.reshape(depth, 1, d), wk, wv, nh)

    (kmat, wst, cpow, lam), w_rounded = _s5_params(
        s5_a_re, s5_a_im, s5_log_dt, s5_b_re, s5_b_im, s5_c_re, s5_c_im, ds, w_next=(w_ffn, 0))
    slot_channel = [_slot_channel(slot, nc) for slot in range(nc)]
    dl = jnp.repeat(s5_d.reshape(na * g, 1, nc)[:, :, slot_channel], ch, axis=2)
    h0r = jnp.swapaxes(state_ssm_re, 1, 2).reshape(na * g, db, p)
    h0i = jnp.swapaxes(state_ssm_im, 1, 2).reshape(na * g, db, p)

    hist_p = jnp.zeros((nb_layers, b, HIST_ROWS, d), F32)
    hist_s = jnp.pad(state_pool, ((0, 0), (0, 0), (HIST_ROWS - POOL_HIST, 0), (0, 0)))
    pb = jnp.reshape(pool_b_grp, (nb_layers, 1, d))
    ps = jnp.reshape(pool_scale, (nb_layers, 1, d))

    ssm_p_re, ssm_p_im, ssm_s_re, ssm_s_im, buf_p, buf_s = [], [], [], [], [], []
    for i in range(depth):
        j = i // 2
        x, w_rounded = ffn(xs if i == 0 else (x,), 2 * i, w_rounded)
        if i % 2 == 0:
            u = _s5_in(x, npre, s5_wt, i, j, nc, tm)
            z, hpr, hpi, hsr, hsi = _s5_scan(
                u, kmat, wst, cpow, lam, dl, h0r, h0i, j, nc, n_prompt // ch, seq // ch, b)
            x = _s5_out(x, z, npost, s5_glu, i, j, nc, tm)
            ssm_p_re.append(jnp.swapaxes(hpr, 0, 1))
            ssm_p_im.append(jnp.swapaxes(hpi, 0, 1))
            ssm_s_re.append(jnp.swapaxes(hsr, 0, 1))
            ssm_s_im.append(jnp.swapaxes(hsi, 0, 1))
        else:
            tp = min(POOL_TILE, seq)
            x, bp = _pool(x, npre, npost, p_win, p_wgrp, pb, ps, hist_p, i, j, 0, seq, tp, 0, tp)
            x, bs = _pool(x, npre, npost, p_win, p_wgrp, pb, ps, hist_s, i, j, n_prompt, ch, ds,
                          PAST_LEN, ch)
            buf_p.append(bp)
            buf_s.append(bs)
        x = _attn(x, npre, npost, wq, wo, kb_p, vb_p, i, nh, seq, min(ATTN_TILE, seq))
        x = _attn_sample(x, npre, npost, wq, wo, cache_mem_k, cache_mem_v, i, n_prompt, ch)
        x, w_rounded = ffn((x,), 2 * i + 1, w_rounded, split_out=(i == depth - 1))

    y_prompt = x[0].reshape(b, seq, d)
    y_sample = x[1].reshape(db, ch, d)[:, :ds, :]
    return (y_prompt, y_sample, mem_k_p, mem_v_p,
            jnp.stack(ssm_p_re), jnp.stack(ssm_p_im), jnp.stack(buf_p),
            jnp.stack(ssm_s_re), jnp.stack(ssm_s_im), jnp.stack(buf_s))
```
